```python
import jax, jax.numpy as jnp
from jax import lax
import numpy as np

D_MODEL = 2048
BATCH = 2
SEQ = 8192
DEPTH = 4

GRID_W = 64
CTX_LEN = 256
HEAD_DIM = 128
MIX_WIDTH = D_MODEL
H_A = 4
KV_A = 2
H_B = 6
H_C = 6
KV_C = 2
D_FF = 4 * D_MODEL
Q_BLOCK = 128
NA_ROWS = 8
NA_COLS = 16
WINDOW = 128
ROPE_THETA = 10000.0
EPS = 1e-6
N_MOD = 6
ATTN_SCALE = HEAD_DIM ** -0.5
NEG_INF = -1e30

_SIZES = (H_A * HEAD_DIM, KV_A * HEAD_DIM, KV_A * HEAD_DIM,
          H_B * HEAD_DIM, H_B * HEAD_DIM, H_B * HEAD_DIM,
          H_C * HEAD_DIM, KV_C * HEAD_DIM, KV_C * HEAD_DIM)
D_IN = sum(_SIZES)
SPLIT_IDX = tuple(int(v) for v in np.cumsum(_SIZES)[:-1])

kernel_name = "hybrid_parallel_heads_dit_trunk"


def _rmsnorm(x, g):
    xf = x.astype(jnp.float32)
    y = xf * lax.rsqrt(jnp.mean(xf * xf, axis=-1, keepdims=True) + EPS)
    return (y * g.astype(jnp.float32)).astype(x.dtype)


def _modulate(h, shift, scale):
    return h * (1 + scale) + shift


def _heads(t):
    return t.reshape(t.shape[:-1] + (t.shape[-1] // HEAD_DIM, HEAD_DIM))


def _axial_rope_tables(length, dtype):
    t = jnp.arange(length)
    row = (t // GRID_W).astype(jnp.float32)
    col = (t % GRID_W).astype(jnp.float32)
    nf = HEAD_DIM // 4
    inv = ROPE_THETA ** (-jnp.arange(nf, dtype=jnp.float32) / nf)
    ar = row[:, None] * inv
    ac = col[:, None] * inv
    ang = jnp.concatenate([ar, ar, ac, ac], axis=-1)
    return jnp.cos(ang)[:, None, :].astype(dtype), jnp.sin(ang)[:, None, :].astype(dtype)


def _rope(x, cos, sin):
    xs = x.reshape(x.shape[:-1] + (2, 2, HEAD_DIM // 4))
    rot = jnp.concatenate([-xs[..., 1:, :], xs[..., :1, :]], axis=-2).reshape(x.shape)
    return x * cos + rot * sin


def _qkv_heads(h, w, qn, kn):
    pa_q, pa_k, pa_v, pb_q, pb_k, pb_v, pc_q, pc_k, pc_v = jnp.split(h @ w, SPLIT_IDX, axis=-1)
    qa = _rmsnorm(_heads(pa_q), qn[0]); ka = _rmsnorm(_heads(pa_k), kn[0]); va = _heads(pa_v)
    qb = _rmsnorm(_heads(pb_q), qn[1]); kb = _rmsnorm(_heads(pb_k), kn[1]); vb = _heads(pb_v)
    qc = _rmsnorm(_heads(pc_q), qn[2]); kc = _rmsnorm(_heads(pc_k), kn[2]); vc = _heads(pc_v)
    return (qa, ka, va, qb, kb, vb, qc, kc, vc)


def _dense_ctx_attn(q, k, v, sink=None):
    b, n, hq, d = q.shape
    hkv = k.shape[2]
    g = hq // hkv
    qg = q.reshape(b, n, hkv, g, d)
    s = jnp.einsum('bqhgd,bkhd->bhgqk', qg, k).astype(jnp.float32) * ATTN_SCALE
    if sink is None:
        p = jax.nn.softmax(s, axis=-1)
    else:
        sk = jnp.broadcast_to(sink.reshape(hkv, g).astype(jnp.float32)[None, :, :, None, None],
                              s.shape[:-1] + (1,))
        p = jax.nn.softmax(jnp.concatenate([s, sk], axis=-1), axis=-1)[..., :-1]
    o = jnp.einsum('bhgqk,bkhd->bqhgd', p.astype(v.dtype), v)
    return o.reshape(b, n, hq * d)


def _global_attn(q, k, v, k_ctx, v_ctx):
    b, L, hq, d = q.shape
    hkv = k.shape[2]
    g = hq // hkv
    nb = L // Q_BLOCK
    k_all = jnp.concatenate([k_ctx, k], axis=1)
    v_all = jnp.concatenate([v_ctx, v], axis=1)
    qb = q.reshape(b, nb, Q_BLOCK, hkv, g, d).transpose(1, 0, 2, 3, 4, 5)

    def block(qi):
        s = jnp.einsum('bqhgd,bkhd->bhgqk', qi, k_all).astype(jnp.float32) * ATTN_SCALE
        p = jax.nn.softmax(s, axis=-1).astype(v_all.dtype)
        return jnp.einsum('bhgqk,bkhd->bqhgd', p, v_all)

    o = lax.map(block, qb)
    return o.transpose(1, 0, 2, 3, 4, 5).reshape(b, L, hq * d)


def _neighbourhood_attn(q, k, v, k_ctx, v_ctx, rpb):
    b, L, h, d = q.shape
    rows = L // GRID_W
    kh = min(NA_ROWS, rows)
    kw = NA_COLS
    qg = q.reshape(b, rows, GRID_W, h, d)
    kg = k.reshape(b, rows, GRID_W, h, d)
    vg = v.reshape(b, rows, GRID_W, h, d)
    cols = np.arange(GRID_W)
    cs = np.clip(cols - kw // 2, 0, GRID_W - kw)
    col_idx = cs[:, None] + np.arange(kw)[None, :]
    col_rel = col_idx - cols[:, None] + (kw - 1)
    rpb_cols = rpb[:, :, col_rel]

    def row(r):
        rs = jnp.clip(r - kh // 2, 0, rows - kh)
        k_win = lax.dynamic_slice_in_dim(kg, rs, kh, axis=1)[:, :, col_idx]
        v_win = lax.dynamic_slice_in_dim(vg, rs, kh, axis=1)[:, :, col_idx]
        q_row = lax.dynamic_index_in_dim(qg, r, axis=1, keepdims=False)
        row_rel = rs + jnp.arange(kh) - r + (NA_ROWS - 1)
        bias = jnp.take(rpb_cols, row_rel, axis=1).transpose(0, 2, 1, 3)
        s_loc = (jnp.einsum('bqhd,biqjhd->bhqij', q_row, k_win).astype(jnp.float32) * ATTN_SCALE
                 + bias[None].astype(jnp.float32))
        s_ctx = jnp.einsum('bqhd,bkhd->bhqk', q_row, k_ctx).astype(jnp.float32) * ATTN_SCALE
        s = jnp.concatenate([s_loc.reshape(b, h, GRID_W, kh * kw), s_ctx], axis=-1)
        p = jax.nn.softmax(s, axis=-1).astype(v.dtype)
        p_loc = p[..., :kh * kw].reshape(b, h, GRID_W, kh, kw)
        p_ctx = p[..., kh * kw:]
        return (jnp.einsum('bhqij,biqjhd->bqhd', p_loc, v_win)
                + jnp.einsum('bhqk,bkhd->bqhd', p_ctx, v_ctx))

    o = lax.map(row, jnp.arange(rows))
    return o.transpose(1, 0, 2, 3, 4).reshape(b, L, h * d)


def _window_attn(q, k, v, k_ctx, v_ctx, sink):
    b, L, hq, d = q.shape
    hkv = k.shape[2]
    g = hq // hkv
    nb = L // WINDOW
    qb = q.reshape(b, nb, WINDOW, hkv, g, d)
    pad = ((0, 0), (WINDOW, WINDOW), (0, 0), (0, 0))
    kp = jnp.pad(k, pad).reshape(b, nb + 2, WINDOW, hkv, d)
    vp = jnp.pad(v, pad).reshape(b, nb + 2, WINDOW, hkv, d)
    k_band = jnp.concatenate([kp[:, :-2], kp[:, 1:-1], kp[:, 2:]], axis=2)
    v_band = jnp.concatenate([vp[:, :-2], vp[:, 1:-1], vp[:, 2:]], axis=2)
    a = np.arange(WINDOW)[:, None]
    kk = np.arange(3 * WINDOW)[None, :]
    in_band = np.abs(kk - WINDOW - a) <= WINDOW
    kpos = (np.arange(nb)[:, None] - 1) * WINDOW + kk
    in_seq = (kpos >= 0) & (kpos < L)
    mask = in_band[None] & in_seq[:, None, :]
    s_band = jnp.einsum('bnqhgd,bnkhd->bnhgqk', qb, k_band).astype(jnp.float32) * ATTN_SCALE
    s_band = jnp.where(mask[None, :, None, None], s_band, NEG_INF)
    s_ctx = jnp.einsum('bnqhgd,bkhd->bnhgqk', qb, k_ctx).astype(jnp.float32) * ATTN_SCALE
    sk = jnp.broadcast_to(sink.reshape(hkv, g).astype(jnp.float32)[None, None, :, :, None, None],
                          s_band.shape[:-1] + (1,))
    p = jax.nn.softmax(jnp.concatenate([s_band, s_ctx, sk], axis=-1), axis=-1)
    nk = 3 * WINDOW
    nc = k_ctx.shape[1]
    p_band = p[..., :nk].astype(v.dtype)
    p_ctx = p[..., nk:nk + nc].astype(v.dtype)
    o = (jnp.einsum('bnhgqk,bnkhd->bnqhgd', p_band, v_band)
         + jnp.einsum('bnhgqk,bkhd->bnqhgd', p_ctx, v_ctx))
    return o.reshape(b, L, hq * d)


def _sq_relu_mlp(h, w_up, w_down):
    return jnp.square(jax.nn.relu(h @ w_up)) @ w_down


def setup_inputs(seed: int = 0) -> dict:
    key = jax.random.key(seed)
    ks = jax.random.split(key, 16)
    n = jax.random.normal
    f32 = jnp.float32
    return {
        "x": n(ks[0], (BATCH, SEQ, D_MODEL), f32),
        "c": n(ks[1], (BATCH, D_MODEL), f32),
        "ctx": n(ks[2], (BATCH, CTX_LEN, D_MODEL), f32),
        "c_ctx": n(ks[3], (D_MODEL,), f32),
        "w_mod": n(ks[4], (DEPTH, D_MODEL, N_MOD * D_MODEL), f32) * (0.5 * D_MODEL ** -0.5),
        "b_mod": n(ks[5], (DEPTH, N_MOD * D_MODEL), f32) * 0.02,
        "norm_attn": 1.0 + 0.02 * n(ks[6], (DEPTH, D_MODEL), f32),
        "norm_mlp": 1.0 + 0.02 * n(ks[7], (DEPTH, D_MODEL), f32),
        "w_in": n(ks[8], (DEPTH, D_MODEL, D_IN), f32) * D_MODEL ** -0.5,
        "q_norm": 1.0 + 0.02 * n(ks[9], (DEPTH, 3, HEAD_DIM), f32),
        "k_norm": 1.0 + 0.02 * n(ks[10], (DEPTH, 3, HEAD_DIM), f32),
        "rel_pos_bias": 0.1 * n(ks[11], (DEPTH, H_B, 2 * NA_ROWS - 1, 2 * NA_COLS - 1), f32),
        "sink_logits": 0.5 * n(ks[12], (DEPTH, H_C), f32),
        "w_out": n(ks[13], (DEPTH, MIX_WIDTH, D_MODEL), f32) * MIX_WIDTH ** -0.5,
        "w_up": n(ks[14], (DEPTH, D_MODEL, D_FF), f32) * D_MODEL ** -0.5,
        "w_down": n(ks[15], (DEPTH, D_FF, D_MODEL), f32) * D_FF ** -0.5,
    }


def reference(x, c, ctx, c_ctx, w_mod, b_mod, norm_attn, norm_mlp, w_in, q_norm, k_norm,
              rel_pos_bias, sink_logits, w_out, w_up, w_down):
    L = x.shape[1]
    cos, sin = _axial_rope_tables(L, x.dtype)
    cond = jax.nn.silu(c)
    cond_ctx = jax.nn.silu(c_ctx)
    cs_ = ctx
    for l in range(DEPTH):
        last = l == DEPTH - 1
        mod = (cond @ w_mod[l] + b_mod[l])[:, None, :]
        sh1, sc1, g1, sh2, sc2, g2 = jnp.split(mod, N_MOD, axis=-1)
        modc = cond_ctx @ w_mod[l] + b_mod[l]
        sh1c, sc1c, g1c, sh2c, sc2c, g2c = jnp.split(modc, N_MOD, axis=-1)

        h = _modulate(_rmsnorm(x, norm_attn[l]), sh1, sc1)
        hc = _modulate(_rmsnorm(cs_, norm_attn[l]), sh1c, sc1c)
        qa, ka, va, qb, kb, vb, qc, kc, vc = _qkv_heads(h, w_in[l], q_norm[l], k_norm[l])
        qa_c, ka_c, va_c, qb_c, kb_c, vb_c, qc_c, kc_c, vc_c = _qkv_heads(hc, w_in[l], q_norm[l], k_norm[l])
        qa = _rope(qa, cos, sin); ka = _rope(ka, cos, sin)
        qc = _rope(qc, cos, sin); kc = _rope(kc, cos, sin)
        ya = _global_attn(qa, ka, va, ka_c, va_c)
        yb = _neighbourhood_attn(qb, kb, vb, kb_c, vb_c, rel_pos_bias[l])
        yc = _window_attn(qc, kc, vc, kc_c, vc_c, sink_logits[l])
        x = x + g1 * (jnp.concatenate([ya, yb, yc], axis=-1) @ w_out[l])
        if not last:
            ya_c = _dense_ctx_attn(qa_c, ka_c, va_c)
            yb_c = _dense_ctx_attn(qb_c, kb_c, vb_c)
            yc_c = _dense_ctx_attn(qc_c, kc_c, vc_c, sink_logits[l])
            cs_ = cs_ + g1c * (jnp.concatenate([ya_c, yb_c, yc_c], axis=-1) @ w_out[l])

        x = x + g2 * _sq_relu_mlp(_modulate(_rmsnorm(x, norm_mlp[l]), sh2, sc2), w_up[l], w_down[l])
        if not last:
            cs_ = cs_ + g2c * _sq_relu_mlp(_modulate(_rmsnorm(cs_, norm_mlp[l]), sh2c, sc2c),
                                           w_up[l], w_down[l])
    return x
```

```python
import functools

import numpy as np
import jax
import jax.numpy as jnp
from jax import lax
from jax.experimental import pallas as pl
from jax.experimental.pallas import tpu as pltpu

HEAD_DIM = 128
H_A, KV_A = 4, 2
H_B = 6
H_C, KV_C = 6, 2
G_A = H_A // KV_A
G_C = H_C // KV_C
GRID_W = 64
NA_ROWS = 8
NA_COLS = 16
WINDOW = 128
ROPE_THETA = 10000.0
EPS = 1e-6
N_MOD = 6
ATTN_SCALE = HEAD_DIM ** -0.5
NEG_INF = -1e30

QA0, KA0, VA0 = 0, H_A, H_A + KV_A
QB0 = VA0 + KV_A
KB0, VB0 = QB0 + H_B, QB0 + 2 * H_B
QC0 = VB0 + H_B
KC0, VC0 = QC0 + H_C, QC0 + H_C + KV_C
N_HEADS_IN = VC0 + KV_C
N_HEADS_OUT = H_A + H_B + H_C

_REGIONS = (
    (QA0, H_A, "q", 0, True), (KA0, KV_A, "k", 0, True), (VA0, KV_A, None, 0, False),
    (QB0, H_B, "q", 1, False), (KB0, H_B, "k", 1, False), (VB0, H_B, None, 0, False),
    (QC0, H_C, "q", 2, True), (KC0, KV_C, "k", 2, True), (VC0, KV_C, None, 0, False),
)

NB_QROWS = 4
NB_KROWS = NB_QROWS + NA_ROWS - 1

VMEM_LIMIT_BYTES = 56 * 1024 * 1024

_BF16 = jnp.bfloat16
_F32 = jnp.float32
_NT = (((1,), (1,)), ((), ()))


def _cparams(*sem):
    return pltpu.CompilerParams(dimension_semantics=sem, vmem_limit_bytes=VMEM_LIMIT_BYTES)


def _norm_mod(x, gain, shift, scale):
    y = x * lax.rsqrt(jnp.mean(x * x, axis=-1, keepdims=True) + EPS)
    return (y * gain) * (1.0 + scale) + shift


def _scores(q, k):
    return lax.dot_general(q, k, _NT, preferred_element_type=_F32) * ATTN_SCALE


def _attend(q, kvs, sink_col=None):
    ss = []
    for k, _, bias, keep in kvs:
        s = _scores(q, k)
        if bias is not None:
            s = s + bias
        if keep is not None:
            s = jnp.where(keep, s, NEG_INF)
        ss.append(s)
    m = ss[0].max(axis=-1, keepdims=True)
    for s in ss[1:]:
        m = jnp.maximum(m, s.max(axis=-1, keepdims=True))
    if sink_col is not None:
        m = jnp.maximum(m, sink_col)
    l = None
    acc = None
    for s, (_, v, _, _) in zip(ss, kvs):
        p = jnp.exp(s - m)
        ls = p.sum(axis=-1, keepdims=True)
        pv = jnp.dot(p.astype(_BF16), v, preferred_element_type=_F32)
        l = ls if l is None else l + ls
        acc = pv if acc is None else acc + pv
    if sink_col is not None:
        l = l + jnp.exp(sink_col - m)
    return acc / l


def _mod_kernel(c_ref, w_ref, b_ref, o_ref):
    c = c_ref[...]
    cond = c * jax.nn.sigmoid(c)
    o_ref[0] = jnp.dot(cond.astype(_BF16), w_ref[0].astype(_BF16),
                       preferred_element_type=_F32) + b_ref[0]


def _modulation(c_rows, w_mod, b_mod):
    depth, d, n = w_mod.shape
    tn = min(n, 512)
    rows = c_rows.shape[0]
    return pl.pallas_call(
        _mod_kernel,
        grid=(depth, n // tn),
        in_specs=[pl.BlockSpec((rows, d), lambda l, j: (0, 0)),
                  pl.BlockSpec((1, d, tn), lambda l, j: (l, 0, j)),
                  pl.BlockSpec((1, 1, tn), lambda l, j: (l, 0, j))],
        out_specs=pl.BlockSpec((1, rows, tn), lambda l, j: (l, 0, j)),
        out_shape=jax.ShapeDtypeStruct((depth, rows, n), _F32),
        compiler_params=_cparams("parallel", "parallel"),
        name="adaln_modulation",
    )(c_rows, w_mod, b_mod.reshape(depth, 1, n))


def _qkv_kernel(x_ref, mod_ref, g_ref, w_ref, qn_ref, kn_ref, *rest, rope):
    if rope:
        cos_ref, sa_ref, sb_ref, o_ref = rest
    else:
        (o_ref,) = rest
    d = x_ref.shape[-1]
    h = _norm_mod(x_ref[0], g_ref[...], mod_ref[0, :, 0:d], mod_ref[0, :, d:2 * d]).astype(_BF16)
    for h0, nh, kind, row, rotary in _REGIONS:
        y = jnp.dot(h, w_ref[:, h0 * HEAD_DIM:(h0 + nh) * HEAD_DIM], preferred_element_type=_F32)
        for j in range(nh):
            yj = y[:, j * HEAD_DIM:(j + 1) * HEAD_DIM]
            if kind is not None:
                gain = (qn_ref if kind == "q" else kn_ref)[row:row + 1, :]
                yj = yj * lax.rsqrt(jnp.mean(yj * yj, axis=-1, keepdims=True) + EPS) * gain
                if rope and rotary:
                    yj = (yj * cos_ref[...]
                          + pltpu.roll(yj, HEAD_DIM - HEAD_DIM // 4, 1) * sa_ref[...]
                          + pltpu.roll(yj, HEAD_DIM // 4, 1) * sb_ref[...])
            o_ref[0, :, (h0 + j) * HEAD_DIM:(h0 + j + 1) * HEAD_DIM] = yj.astype(_BF16)


def _qkv_proj(x, mod, mod_row, gain, w_in, qn, kn, rope_tabs, tm):
    bsz, length, d = x.shape
    d_in = w_in.shape[1]
    tm = min(tm, length)
    if mod_row is None:
        mod_map = lambda b, i: (b, 0, 0)
    else:
        mod_map = lambda b, i: (mod_row, 0, 0)
    in_specs = [
        pl.BlockSpec((1, tm, d), lambda b, i: (b, i, 0)),
        pl.BlockSpec((1, 1, mod.shape[-1]), mod_map),
        pl.BlockSpec((1, d), lambda b, i: (0, 0)),
        pl.BlockSpec((d, d_in), lambda b, i: (0, 0), pipeline_mode=pl.Buffered(1)),
        pl.BlockSpec(qn.shape, lambda b, i: (0, 0)),
        pl.BlockSpec(kn.shape, lambda b, i: (0, 0)),
    ]
    args = [x, mod, gain, w_in, qn, kn]
    if rope_tabs is not None:
        in_specs += [pl.BlockSpec((tm, HEAD_DIM), lambda b, i: (i, 0))] * 3
        args += list(rope_tabs)
    return pl.pallas_call(
        functools.partial(_qkv_kernel, rope=rope_tabs is not None),
        grid=(bsz, length // tm),
        in_specs=in_specs,
        out_specs=pl.BlockSpec((1, tm, d_in), lambda b, i: (b, i, 0)),
        out_shape=jax.ShapeDtypeStruct((bsz, length, d_in), _BF16),
        compiler_params=_cparams("parallel", "parallel"),
        name="qkv_proj_rope" if rope_tabs is not None else "qkv_proj_ctx",
    )(*args)


def _attn_a_kernel(q_ref, k_ref, v_ref, kc_ref, vc_ref, o_ref, m_sc, l_sc, acc_sc, *, tk):
    tq = q_ref.shape[1]
    length = k_ref.shape[1]
    q = q_ref[0]
    q2 = jnp.concatenate([q[:, g * HEAD_DIM:(g + 1) * HEAD_DIM] for g in range(G_A)], axis=0)

    s = _scores(q2, kc_ref[0])
    m = s.max(axis=-1, keepdims=True)
    p = jnp.exp(s - m)
    m_sc[...] = m
    l_sc[...] = p.sum(axis=-1, keepdims=True)
    acc_sc[...] = jnp.dot(p.astype(_BF16), vc_ref[0], preferred_element_type=_F32)

    def body(j, carry):
        off = pl.multiple_of(j * tk, tk)
        s = _scores(q2, k_ref[0, pl.ds(off, tk), :])
        m_prev = m_sc[...]
        m_new = jnp.maximum(m_prev, s.max(axis=-1, keepdims=True))
        alpha = jnp.exp(m_prev - m_new)
        p = jnp.exp(s - m_new)
        l_sc[...] = alpha * l_sc[...] + p.sum(axis=-1, keepdims=True)
        acc_sc[...] = alpha * acc_sc[...] + jnp.dot(
            p.astype(_BF16), v_ref[0, pl.ds(off, tk), :], preferred_element_type=_F32)
        m_sc[...] = m_new
        return carry

    lax.fori_loop(0, length // tk, body, 0)
    o = acc_sc[...] / l_sc[...]
    o_ref[0] = jnp.concatenate([o[g * tq:(g + 1) * tq] for g in range(G_A)], axis=1).astype(_BF16)


def _attn_a(qkv, qkv_ctx, tq, tk):
    bsz, length, _ = qkv.shape
    n_ctx = qkv_ctx.shape[1]
    tq = min(tq, length)
    tk = min(tk, length)
    return pl.pallas_call(
        functools.partial(_attn_a_kernel, tk=tk),
        grid=(bsz, KV_A, length // tq),
        in_specs=[
            pl.BlockSpec((1, tq, G_A * HEAD_DIM), lambda b, h, i: (b, i, h)),
            pl.BlockSpec((1, length, HEAD_DIM), lambda b, h, i: (b, 0, KA0 + h)),
            pl.BlockSpec((1, length, HEAD_DIM), lambda b, h, i: (b, 0, VA0 + h)),
            pl.BlockSpec((1, n_ctx, HEAD_DIM), lambda b, h, i: (b, 0, KA0 + h)),
            pl.BlockSpec((1, n_ctx, HEAD_DIM), lambda b, h, i: (b, 0, VA0 + h)),
        ],
        out_specs=pl.BlockSpec((1, tq, G_A * HEAD_DIM), lambda b, h, i: (b, i, h)),
        out_shape=jax.ShapeDtypeStruct((bsz, length, H_A * HEAD_DIM), _BF16),
        scratch_shapes=[pltpu.VMEM((G_A * tq, 1), _F32), pltpu.VMEM((G_A * tq, 1), _F32),
                        pltpu.VMEM((G_A * tq, HEAD_DIM), _F32)],
        compiler_params=_cparams("parallel", "parallel", "parallel"),
        name="attn_global",
    )(qkv, qkv, qkv, qkv_ctx, qkv_ctx)


def _attn_b_kernel(q_ref, k_ref, v_ref, kc_ref, vc_ref, bias_ref, o_ref):
    t = pl.program_id(2)
    n_rows = k_ref.shape[1] // GRID_W
    nk = NB_KROWS * GRID_W
    ks = jnp.clip(t * NB_QROWS - NA_ROWS // 2, 0, n_rows - NB_KROWS)
    off = pl.multiple_of(ks * GRID_W, GRID_W)
    kb = k_ref[0, pl.ds(off, nk), :]
    vb = v_ref[0, pl.ds(off, nk), :]
    o = _attend(q_ref[0], [(kb, vb, bias_ref[0, 0], None), (kc_ref[0], vc_ref[0], None, None)])
    o_ref[0] = o.astype(_BF16)


def _nb_bias_tables(rpb, n_rows):
    kh = NA_ROWS
    a = np.arange(NB_QROWS)[:, None, None, None]
    c = np.arange(GRID_W)[None, :, None, None]
    i = np.arange(NB_KROWS)[None, None, :, None]
    ck = np.arange(GRID_W)[None, None, None, :]
    cs = np.clip(c - NA_COLS // 2, 0, GRID_W - NA_COLS)
    col_ok = (ck >= cs) & (ck < cs + NA_COLS)
    col_rel = np.clip(ck - c + (NA_COLS - 1), 0, 2 * NA_COLS - 2)
    rr, cr, ok = [], [], []
    for r0, ks in ((0, 0), (NB_QROWS, 0), (n_rows - NB_QROWS, n_rows - NB_KROWS)):
        rq = r0 + a
        rk = ks + i
        rs = np.clip(rq - kh // 2, 0, n_rows - kh)
        row_ok = (rk >= rs) & (rk < rs + kh)
        row_rel = np.clip(rk - rq + (NA_ROWS - 1), 0, 2 * NA_ROWS - 2)
        shape = (NB_QROWS, GRID_W, NB_KROWS, GRID_W)
        flat = (NB_QROWS * GRID_W, NB_KROWS * GRID_W)
        rr.append(np.broadcast_to(row_rel, shape).reshape(flat))
        cr.append(np.broadcast_to(col_rel, shape).reshape(flat))
        ok.append(np.broadcast_to(row_ok & col_ok, shape).reshape(flat))
    rr, cr, ok = np.stack(rr), np.stack(cr), np.stack(ok)
    return jnp.where(ok[None], rpb[:, rr, cr], NEG_INF).astype(_F32)


def _attn_b(qkv, qkv_ctx, bias_tabs):
    bsz, length, _ = qkv.shape
    n_ctx = qkv_ctx.shape[1]
    tq = NB_QROWS * GRID_W
    nt = length // tq
    nk = NB_KROWS * GRID_W

    def bias_map(b, h, t):
        return (h, jnp.where(t == 0, 0, jnp.where(t == nt - 1, 2, 1)), 0, 0)

    return pl.pallas_call(
        _attn_b_kernel,
        grid=(bsz, H_B, nt),
        in_specs=[
            pl.BlockSpec((1, tq, HEAD_DIM), lambda b, h, t: (b, t, QB0 + h)),
            pl.BlockSpec((1, length, HEAD_DIM), lambda b, h, t: (b, 0, KB0 + h)),
            pl.BlockSpec((1, length, HEAD_DIM), lambda b, h, t: (b, 0, VB0 + h)),
            pl.BlockSpec((1, n_ctx, HEAD_DIM), lambda b, h, t: (b, 0, KB0 + h)),
            pl.BlockSpec((1, n_ctx, HEAD_DIM), lambda b, h, t: (b, 0, VB0 + h)),
            pl.BlockSpec((1, 1, tq, nk), bias_map),
        ],
        out_specs=pl.BlockSpec((1, tq, HEAD_DIM), lambda b, h, t: (b, t, h)),
        out_shape=jax.ShapeDtypeStruct((bsz, length, H_B * HEAD_DIM), _BF16),
        compiler_params=_cparams("parallel", "parallel", "parallel"),
        name="attn_neighbourhood",
    )(qkv, qkv, qkv, qkv_ctx, qkv_ctx, bias_tabs)


def _sink_column(sink_ref, rows_per_head):
    sink = sink_ref[0]
    return jnp.concatenate(
        [jnp.broadcast_to(sink[g:g + 1, :1], (rows_per_head, 1)) for g in range(G_C)], axis=0)


def _attn_c_kernel(q0_ref, q1_ref, q2_ref, k_ref, v_ref, kc_ref, vc_ref, sink_ref, o_ref):
    t = pl.program_id(2)
    tq = q0_ref.shape[1]
    length = k_ref.shape[1]
    band = 3 * WINDOW
    kc = kc_ref[0]
    vc = vc_ref[0]
    sink_col = _sink_column(sink_ref, WINDOW)
    rel = (lax.broadcasted_iota(jnp.int32, (G_C * WINDOW, band), 0) % WINDOW
           - lax.broadcasted_iota(jnp.int32, (G_C * WINDOW, band), 1))
    for n in range(tq // WINDOW):
        i0 = t * tq + n * WINDOW
        start = pl.multiple_of(jnp.clip(i0 - WINDOW, 0, length - band), WINDOW)
        kb = k_ref[0, pl.ds(start, band), :]
        vb = v_ref[0, pl.ds(start, band), :]
        q3 = jnp.concatenate([r[0, n * WINDOW:(n + 1) * WINDOW, :] for r in (q0_ref, q1_ref, q2_ref)],
                             axis=0)
        keep = jnp.abs(rel + (i0 - start)) <= WINDOW
        o = _attend(q3, [(kb, vb, None, keep), (kc, vc, None, None)], sink_col)
        o_ref[0, n * WINDOW:(n + 1) * WINDOW, :] = jnp.concatenate(
            [o[g * WINDOW:(g + 1) * WINDOW] for g in range(G_C)], axis=1).astype(_BF16)


def _attn_c(qkv, qkv_ctx, sink_tab, tq):
    bsz, length, _ = qkv.shape
    n_ctx = qkv_ctx.shape[1]
    tq = min(tq, length)
    q_specs = [pl.BlockSpec((1, tq, HEAD_DIM), functools.partial(
        lambda b, h, t, g: (b, t, QC0 + h * G_C + g), g=g)) for g in range(G_C)]
    return pl.pallas_call(
        _attn_c_kernel,
        grid=(bsz, KV_C, length // tq),
        in_specs=q_specs + [
            pl.BlockSpec((1, length, HEAD_DIM), lambda b, h, t: (b, 0, KC0 + h)),
            pl.BlockSpec((1, length, HEAD_DIM), lambda b, h, t: (b, 0, VC0 + h)),
            pl.BlockSpec((1, n_ctx, HEAD_DIM), lambda b, h, t: (b, 0, KC0 + h)),
            pl.BlockSpec((1, n_ctx, HEAD_DIM), lambda b, h, t: (b, 0, VC0 + h)),
            pl.BlockSpec((1, G_C, HEAD_DIM), lambda b, h, t: (h, 0, 0)),
        ],
        out_specs=pl.BlockSpec((1, tq, G_C * HEAD_DIM), lambda b, h, t: (b, t, h)),
        out_shape=jax.ShapeDtypeStruct((bsz, length, H_C * HEAD_DIM), _BF16),
        compiler_params=_cparams("parallel", "parallel", "parallel"),
        name="attn_window",
    )(qkv, qkv, qkv, qkv, qkv, qkv_ctx, qkv_ctx, sink_tab)


def _attn_ctx_kernel(qkv_ref, sink_ref, ya_ref, yb_ref, yc_ref):
    n = qkv_ref.shape[1]

    def head(u):
        return qkv_ref[0, :, u * HEAD_DIM:(u + 1) * HEAD_DIM]

    for kv in range(KV_A):
        q = jnp.concatenate([head(QA0 + kv * G_A + g) for g in range(G_A)], axis=0)
        o = _attend(q, [(head(KA0 + kv), head(VA0 + kv), None, None)])
        for g in range(G_A):
            u = kv * G_A + g
            ya_ref[0, :, u * HEAD_DIM:(u + 1) * HEAD_DIM] = o[g * n:(g + 1) * n].astype(_BF16)
    for hb in range(H_B):
        o = _attend(head(QB0 + hb), [(head(KB0 + hb), head(VB0 + hb), None, None)])
        yb_ref[0, :, hb * HEAD_DIM:(hb + 1) * HEAD_DIM] = o.astype(_BF16)
    for kv in range(KV_C):
        q = jnp.concatenate([head(QC0 + kv * G_C + g) for g in range(G_C)], axis=0)
        sink = sink_ref[kv]
        sink_col = jnp.concatenate(
            [jnp.broadcast_to(sink[g:g + 1, :1], (n, 1)) for g in range(G_C)], axis=0)
        o = _attend(q, [(head(KC0 + kv), head(VC0 + kv), None, None)], sink_col)
        for g in range(G_C):
            u = kv * G_C + g
            yc_ref[0, :, u * HEAD_DIM:(u + 1) * HEAD_DIM] = o[g * n:(g + 1) * n].astype(_BF16)


def _attn_ctx(qkv_ctx, sink_tab):
    bsz, n, d_in = qkv_ctx.shape
    widths = (H_A * HEAD_DIM, H_B * HEAD_DIM, H_C * HEAD_DIM)
    return pl.pallas_call(
        _attn_ctx_kernel,
        grid=(bsz,),
        in_specs=[pl.BlockSpec((1, n, d_in), lambda b: (b, 0, 0)),
                  pl.BlockSpec(sink_tab.shape, lambda b: (0, 0, 0))],
        out_specs=[pl.BlockSpec((1, n, w), lambda b: (b, 0, 0)) for w in widths],
        out_shape=[jax.ShapeDtypeStruct((bsz, n, w), _BF16) for w in widths],
        compiler_params=_cparams("parallel"),
        name="attn_ctx",
    )(qkv_ctx, sink_tab)


def _outproj_kernel(ya_ref, yb_ref, yc_ref, x_ref, mod_ref, g_ref, w_ref, xo_ref, h_ref):
    d = x_ref.shape[-1]
    acc = None
    r0 = 0
    for y_ref in (ya_ref, yb_ref, yc_ref):
        r1 = r0 + y_ref.shape[-1]
        part = jnp.dot(y_ref[0], w_ref[r0:r1, :], preferred_element_type=_F32)
        acc = part if acc is None else acc + part
        r0 = r1
    xn = x_ref[0] + mod_ref[0, :, 2 * d:3 * d] * acc
    xo_ref[0] = xn
    h_ref[0] = _norm_mod(xn, g_ref[...], mod_ref[0, :, 3 * d:4 * d],
                         mod_ref[0, :, 4 * d:5 * d]).astype(_BF16)


def _outproj(ya, yb, yc, x, mod, mod_row, gain, w_out, tm):
    bsz, length, d = x.shape
    tm = min(tm, length)
    if mod_row is None:
        mod_map = lambda b, i: (b, 0, 0)
    else:
        mod_map = lambda b, i: (mod_row, 0, 0)
    row = lambda b, i: (b, i, 0)
    return pl.pallas_call(
        _outproj_kernel,
        grid=(bsz, length // tm),
        in_specs=[pl.BlockSpec((1, tm, y.shape[-1]), row) for y in (ya, yb, yc)] + [
            pl.BlockSpec((1, tm, d), row),
            pl.BlockSpec((1, 1, mod.shape[-1]), mod_map),
            pl.BlockSpec((1, d), lambda b, i: (0, 0)),
            pl.BlockSpec(w_out.shape, lambda b, i: (0, 0), pipeline_mode=pl.Buffered(1)),
        ],
        out_specs=[pl.BlockSpec((1, tm, d), row), pl.BlockSpec((1, tm, d), row)],
        out_shape=[jax.ShapeDtypeStruct((bsz, length, d), _F32),
                   jax.ShapeDtypeStruct((bsz, length, d), _BF16)],
        compiler_params=_cparams("parallel", "parallel"),
        name="out_proj_residual",
    )(ya, yb, yc, x, mod, gain, w_out)


def _mlp_kernel(h_ref, x_ref, mod_ref, wu_ref, wd_ref, o_ref):
    k = pl.program_id(2)
    d = x_ref.shape[-1]
    u = jnp.dot(h_ref[0], wu_ref[...], preferred_element_type=_F32)
    act = jnp.square(jnp.maximum(u, 0.0)).astype(_BF16)
    y = jnp.dot(act, wd_ref[...], preferred_element_type=_F32)

    @pl.when(k == 0)
    def _():
        o_ref[0] = y

    @pl.when(k > 0)
    def _():
        o_ref[0] += y

    @pl.when(k == pl.num_programs(2) - 1)
    def _():
        o_ref[0] = x_ref[0] + mod_ref[0, :, 5 * d:6 * d] * o_ref[0]


def _mlp(h, x, mod, mod_row, w_up, w_down, tm, tf):
    bsz, length, d = x.shape
    d_ff = w_up.shape[1]
    tm = min(tm, length)
    tf = min(tf, d_ff)
    if mod_row is None:
        mod_map = lambda b, i, k: (b, 0, 0)
    else:
        mod_map = lambda b, i, k: (mod_row, 0, 0)
    row = lambda b, i, k: (b, i, 0)
    return pl.pallas_call(
        _mlp_kernel,
        grid=(bsz, length // tm, d_ff // tf),
        in_specs=[pl.BlockSpec((1, tm, d), row),
                  pl.BlockSpec((1, tm, d), row),
                  pl.BlockSpec((1, 1, mod.shape[-1]), mod_map),
                  pl.BlockSpec((d, tf), lambda b, i, k: (0, k)),
                  pl.BlockSpec((tf, d), lambda b, i, k: (k, 0))],
        out_specs=pl.BlockSpec((1, tm, d), row),
        out_shape=jax.ShapeDtypeStruct((bsz, length, d), _F32),
        compiler_params=_cparams("parallel", "parallel", "arbitrary"),
        name="mlp_sq_relu",
    )(h, x, mod, w_up, w_down)


def _rope_tables(length):
    t = np.arange(length)
    nf = HEAD_DIM // 4
    inv = ROPE_THETA ** (-jnp.arange(nf, dtype=_F32) / nf)
    ar = jnp.asarray(t // GRID_W, _F32)[:, None] * inv
    ac = jnp.asarray(t % GRID_W, _F32)[:, None] * inv
    ang = jnp.concatenate([ar, ar, ac, ac], axis=-1)
    cos, sin = jnp.cos(ang), jnp.sin(ang)
    first = (np.arange(HEAD_DIM) // nf) % 2 == 0
    return cos, jnp.where(first, -sin, 0.0), jnp.where(first, 0.0, sin)


def kernel(x, c, ctx, c_ctx, w_mod, b_mod, norm_attn, norm_mlp, w_in, q_norm, k_norm,
           rel_pos_bias, sink_logits, w_out, w_up, w_down):
    bsz, length, d = x.shape
    depth = w_mod.shape[0]
    assert length % (NB_QROWS * GRID_W) == 0 and length // GRID_W >= NB_KROWS
    assert w_in.shape[-1] == N_HEADS_IN * HEAD_DIM and w_out.shape[1] == N_HEADS_OUT * HEAD_DIM

    mod_rows = 8
    c_rows = jnp.zeros((mod_rows, d), _F32).at[:bsz].set(c).at[bsz].set(c_ctx)
    mod = _modulation(c_rows, w_mod, b_mod).reshape(depth, mod_rows, 1, N_MOD * d)

    w_in_b, w_out_b = w_in.astype(_BF16), w_out.astype(_BF16)
    w_up_b, w_down_b = w_up.astype(_BF16), w_down.astype(_BF16)
    rope_tabs = _rope_tables(length)
    sink_tabs = jnp.broadcast_to(
        sink_logits.reshape(depth, KV_C, G_C, 1), (depth, KV_C, G_C, HEAD_DIM)).astype(_F32)

    cs = ctx
    for l in range(depth):
        last = l == depth - 1
        g_attn = norm_attn[l].reshape(1, d)
        g_mlp = norm_mlp[l].reshape(1, d)
        qkv = _qkv_proj(x, mod[l], None, g_attn, w_in_b[l], q_norm[l], k_norm[l], rope_tabs, tm=256)
        qkv_c = _qkv_proj(cs, mod[l], bsz, g_attn, w_in_b[l], q_norm[l], k_norm[l], None, tm=256)
        ya = _attn_a(qkv, qkv_c, tq=256, tk=512)
        yb = _attn_b(qkv, qkv_c, _nb_bias_tables(rel_pos_bias[l], length // GRID_W))
        yc = _attn_c(qkv, qkv_c, sink_tabs[l], tq=512)
        x, h = _outproj(ya, yb, yc, x, mod[l], None, g_mlp, w_out_b[l], tm=512)
        x = _mlp(h, x, mod[l], None, w_up_b[l], w_down_b[l], tm=512, tf=512)
        if not last:
            ya_c, yb_c, yc_c = _attn_ctx(qkv_c, sink_tabs[l])
            cs, hc = _outproj(ya_c, yb_c, yc_c, cs, mod[l], bsz, g_mlp, w_out_b[l], tm=256)
            cs = _mlp(hc, cs, mod[l], bsz, w_up_b[l], w_down_b[l], tm=256, tf=512)
    return x
```

```python
import functools

import numpy as np
import jax
import jax.numpy as jnp
from jax import lax
from jax.experimental import pallas as pl
from jax.experimental.pallas import tpu as pltpu

HEAD_DIM = 128
H_A, KV_A = 4, 2
H_B = 6
H_C, KV_C = 6, 2
G_A = H_A // KV_A
G_C = H_C // KV_C
GRID_W = 64
NA_ROWS = 8
NA_COLS = 16
WINDOW = 128
ROPE_THETA = 10000.0
EPS = 1e-6
N_MOD = 6
ATTN_SCALE = HEAD_DIM ** -0.5
NEG_INF = -1e30

QA0, KA0, VA0 = 0, H_A, H_A + KV_A
QB0 = VA0 + KV_A
KB0, VB0 = QB0 + H_B, QB0 + 2 * H_B
QC0 = VB0 + H_B
KC0, VC0 = QC0 + H_C, QC0 + H_C + KV_C
N_HEADS_IN = VC0 + KV_C
N_HEADS_OUT = H_A + H_B + H_C

_REGIONS = (
    (QA0, H_A, "q", 0, True), (KA0, KV_A, "k", 0, True), (VA0, KV_A, None, 0, False),
    (QB0, H_B, "q", 1, False), (KB0, H_B, "k", 1, False), (VB0, H_B, None, 0, False),
    (QC0, H_C, "q", 2, True), (KC0, KV_C, "k", 2, True), (VC0, KV_C, None, 0, False),
)

NB_QROWS = 4
NB_KROWS = NB_QROWS + NA_ROWS - 1

VMEM_LIMIT_BYTES = 56 * 1024 * 1024

_BF16 = jnp.bfloat16
_F32 = jnp.float32
_NT = (((1,), (1,)), ((), ()))
_TN = (((0,), (0,)), ((), ()))
LOG2_E = 1.4426950408889634


def _cparams(*sem):
    return pltpu.CompilerParams(dimension_semantics=sem, vmem_limit_bytes=VMEM_LIMIT_BYTES)


def _norm_mod(x, gain, shift, scale):
    y = x * lax.rsqrt(jnp.mean(x * x, axis=-1, keepdims=True) + EPS)
    return (y * gain) * (1.0 + scale) + shift


def _scores(q, k):
    return lax.dot_general(q, k, _NT, preferred_element_type=_F32) * ATTN_SCALE


def _attend(q, kvs, sink_col=None):
    ss = []
    for k, _, bias, keep in kvs:
        s = _scores(q, k)
        if bias is not None:
            s = s + bias
        if keep is not None:
            s = jnp.where(keep, s, NEG_INF)
        ss.append(s)
    m = ss[0].max(axis=-1, keepdims=True)
    for s in ss[1:]:
        m = jnp.maximum(m, s.max(axis=-1, keepdims=True))
    if sink_col is not None:
        m = jnp.maximum(m, sink_col)
    l = None
    acc = None
    for s, (_, v, _, _) in zip(ss, kvs):
        p = jnp.exp(s - m)
        ls = p.sum(axis=-1, keepdims=True)
        pv = jnp.dot(p.astype(_BF16), v, preferred_element_type=_F32)
        l = ls if l is None else l + ls
        acc = pv if acc is None else acc + pv
    if sink_col is not None:
        l = l + jnp.exp(sink_col - m)
    return acc / l


def _mod_kernel(c_ref, w_ref, b_ref, o_ref):
    c = c_ref[...]
    cond = c * jax.nn.sigmoid(c)
    o_ref[0] = jnp.dot(cond.astype(_BF16), w_ref[0].astype(_BF16),
                       preferred_element_type=_F32) + b_ref[0]


def _modulation(c_rows, w_mod, b_mod):
    depth, d, n = w_mod.shape
    tn = min(n, 512)
    rows = c_rows.shape[0]
    return pl.pallas_call(
        _mod_kernel,
        grid=(depth, n // tn),
        in_specs=[pl.BlockSpec((rows, d), lambda l, j: (0, 0)),
                  pl.BlockSpec((1, d, tn), lambda l, j: (l, 0, j)),
                  pl.BlockSpec((1, 1, tn), lambda l, j: (l, 0, j))],
        out_specs=pl.BlockSpec((1, rows, tn), lambda l, j: (l, 0, j)),
        out_shape=jax.ShapeDtypeStruct((depth, rows, n), _F32),
        compiler_params=_cparams("parallel", "parallel"),
        name="adaln_modulation",
    )(c_rows, w_mod, b_mod.reshape(depth, 1, n))


def _qkv_kernel(x_ref, mod_ref, g_ref, w_ref, qn_ref, kn_ref, *rest, rope):
    if rope:
        cos_ref, sa_ref, sb_ref, o_ref = rest
    else:
        (o_ref,) = rest
    d = x_ref.shape[-1]
    h = _norm_mod(x_ref[0], g_ref[...], mod_ref[0, :, 0:d], mod_ref[0, :, d:2 * d]).astype(_BF16)
    for h0, nh, kind, row, rotary in _REGIONS:
        y = jnp.dot(h, w_ref[:, h0 * HEAD_DIM:(h0 + nh) * HEAD_DIM], preferred_element_type=_F32)
        for j in range(nh):
            yj = y[:, j * HEAD_DIM:(j + 1) * HEAD_DIM]
            if kind is not None:
                gain = (qn_ref if kind == "q" else kn_ref)[row:row + 1, :]
                yj = yj * lax.rsqrt(jnp.mean(yj * yj, axis=-1, keepdims=True) + EPS) * gain
                if rope and rotary:
                    yj = (yj * cos_ref[...]
                          + pltpu.roll(yj, HEAD_DIM - HEAD_DIM // 4, 1) * sa_ref[...]
                          + pltpu.roll(yj, HEAD_DIM // 4, 1) * sb_ref[...])
            o_ref[0, :, (h0 + j) * HEAD_DIM:(h0 + j + 1) * HEAD_DIM] = yj.astype(_BF16)


def _qkv_proj(x, mod, mod_row, gain, w_in, qn, kn, rope_tabs, tm):
    bsz, length, d = x.shape
    d_in = w_in.shape[1]
    tm = min(tm, length)
    if mod_row is None:
        mod_map = lambda b, i: (b, 0, 0)
    else:
        mod_map = lambda b, i: (mod_row, 0, 0)
    in_specs = [
        pl.BlockSpec((1, tm, d), lambda b, i: (b, i, 0)),
        pl.BlockSpec((1, 1, mod.shape[-1]), mod_map),
        pl.BlockSpec((1, d), lambda b, i: (0, 0)),
        pl.BlockSpec((d, d_in), lambda b, i: (0, 0), pipeline_mode=pl.Buffered(1)),
        pl.BlockSpec(qn.shape, lambda b, i: (0, 0)),
        pl.BlockSpec(kn.shape, lambda b, i: (0, 0)),
    ]
    args = [x, mod, gain, w_in, qn, kn]
    if rope_tabs is not None:
        in_specs += [pl.BlockSpec((tm, HEAD_DIM), lambda b, i: (i, 0))] * 3
        args += list(rope_tabs)
    return pl.pallas_call(
        functools.partial(_qkv_kernel, rope=rope_tabs is not None),
        grid=(bsz, length // tm),
        in_specs=in_specs,
        out_specs=pl.BlockSpec((1, tm, d_in), lambda b, i: (b, i, 0)),
        out_shape=jax.ShapeDtypeStruct((bsz, length, d_in), _BF16),
        compiler_params=_cparams("parallel", "parallel"),
        name="qkv_proj_rope" if rope_tabs is not None else "qkv_proj_ctx",
    )(*args)


def _attn_a_kernel(q_ref, k_ref, v_ref, kc_ref, vc_ref, o_ref, acc_sc, s0_sc, s1_sc, *, tk):
    tq = q_ref.shape[1]
    length = k_ref.shape[1]
    nq = G_A * tq
    q = q_ref[0]
    q2 = jnp.concatenate([q[:, g * HEAD_DIM:(g + 1) * HEAD_DIM] for g in range(G_A)], axis=0)
    c = ATTN_SCALE * LOG2_E

    def qk(kb):
        return lax.dot_general(kb, q2, _NT, preferred_element_type=_F32)

    def softmax_pv(s, vb, m_prev, l_prev):
        m_new = jnp.maximum(m_prev, s.max(axis=0, keepdims=True))
        alpha = jnp.exp2(c * (m_prev - m_new))
        p = jnp.exp2(c * s - c * m_new)
        l_new = alpha * l_prev + p.sum(axis=0, keepdims=True)
        pv = lax.dot_general(vb, p.astype(_BF16), _TN, preferred_element_type=_F32)
        acc_sc[...] = alpha * acc_sc[...] + pv
        return m_new, l_new

    def keys(ref, j):
        return ref[0, pl.ds(pl.multiple_of(j * tk, tk), tk), :]

    n_blocks = length // tk
    acc_sc[...] = jnp.zeros_like(acc_sc)
    m0 = jnp.full((1, nq), NEG_INF, _F32)
    l0 = jnp.zeros((1, nq), _F32)
    s0_sc[...] = qk(keys(k_ref, 0))
    m, l = softmax_pv(qk(kc_ref[0]), vc_ref[0], m0, l0)

    def pair(j0, carry, last):
        s1_sc[...] = qk(keys(k_ref, j0 + 1))
        carry = softmax_pv(s0_sc[...], keys(v_ref, j0), *carry)
        if not last:
            s0_sc[...] = qk(keys(k_ref, j0 + 2))
        return softmax_pv(s1_sc[...], keys(v_ref, j0 + 1), *carry)

    m, l = lax.fori_loop(0, n_blocks // 2 - 1, lambda jj, carry: pair(2 * jj, carry, False), (m, l))
    m, l = pair(n_blocks - 2, (m, l), True)
    o = (acc_sc[...] / l).T
    o_ref[0] = jnp.concatenate([o[g * tq:(g + 1) * tq] for g in range(G_A)], axis=1).astype(_BF16)


def _attn_a(qkv, qkv_ctx, tq, tk):
    bsz, length, _ = qkv.shape
    n_ctx = qkv_ctx.shape[1]
    tq = min(tq, length)
    tk = min(tk, length // 2)
    assert length % (2 * tk) == 0
    return pl.pallas_call(
        functools.partial(_attn_a_kernel, tk=tk),
        grid=(bsz, KV_A, length // tq),
        in_specs=[
            pl.BlockSpec((1, tq, G_A * HEAD_DIM), lambda b, h, i: (b, i, h)),
            pl.BlockSpec((1, length, HEAD_DIM), lambda b, h, i: (b, 0, KA0 + h)),
            pl.BlockSpec((1, length, HEAD_DIM), lambda b, h, i: (b, 0, VA0 + h)),
            pl.BlockSpec((1, n_ctx, HEAD_DIM), lambda b, h, i: (b, 0, KA0 + h)),
            pl.BlockSpec((1, n_ctx, HEAD_DIM), lambda b, h, i: (b, 0, VA0 + h)),
        ],
        out_specs=pl.BlockSpec((1, tq, G_A * HEAD_DIM), lambda b, h, i: (b, i, h)),
        out_shape=jax.ShapeDtypeStruct((bsz, length, H_A * HEAD_DIM), _BF16),
        scratch_shapes=[pltpu.VMEM((HEAD_DIM, G_A * tq), _F32),
                        pltpu.VMEM((tk, G_A * tq), _F32), pltpu.VMEM((tk, G_A * tq), _F32)],
        compiler_params=_cparams("parallel", "parallel", "parallel"),
        name="attn_global",
    )(qkv, qkv, qkv, qkv_ctx, qkv_ctx)


def _attn_b_kernel(q_ref, k_ref, v_ref, kc_ref, vc_ref, bias_ref, o_ref):
    t = pl.program_id(2)
    n_rows = k_ref.shape[1] // GRID_W
    nk = NB_KROWS * GRID_W
    ks = jnp.clip(t * NB_QROWS - NA_ROWS // 2, 0, n_rows - NB_KROWS)
    off = pl.multiple_of(ks * GRID_W, GRID_W)
    kb = k_ref[0, pl.ds(off, nk), :]
    vb = v_ref[0, pl.ds(off, nk), :]
    o = _attend(q_ref[0], [(kb, vb, bias_ref[0, 0], None), (kc_ref[0], vc_ref[0], None, None)])
    o_ref[0] = o.astype(_BF16)


def _nb_bias_tables(rpb, n_rows):
    kh = NA_ROWS
    c = np.arange(GRID_W)[:, None, None]
    j = np.arange(2 * NA_COLS - 1)[None, :, None]
    ck = np.arange(GRID_W)[None, None, :]
    col_onehot = (j == ck - c + (NA_COLS - 1)).astype(np.float32)
    cs = np.clip(c - NA_COLS // 2, 0, GRID_W - NA_COLS)
    col_ok = ((ck >= cs) & (ck < cs + NA_COLS))[:, 0, :]
    a = np.arange(NB_QROWS)[:, None, None]
    i = np.arange(NB_KROWS)[None, :, None]
    r = np.arange(2 * NA_ROWS - 1)[None, None, :]
    row_onehot, row_ok = [], []
    for r0, ks in ((0, 0), (NB_QROWS, 0), (n_rows - NB_QROWS, n_rows - NB_KROWS)):
        rq = r0 + a
        rk = ks + i
        rs = np.clip(rq - kh // 2, 0, n_rows - kh)
        row_ok.append(((rk >= rs) & (rk < rs + kh))[:, :, 0])
        row_onehot.append((r == rk - rq + (NA_ROWS - 1)).astype(np.float32))
    row_onehot, row_ok = np.stack(row_onehot), np.stack(row_ok)
    hi = lax.Precision.HIGHEST
    t1 = jnp.einsum("lhrj,cjk->lhrck", rpb, col_onehot, precision=hi)
    tab = jnp.einsum("lhrck,vair->lhvacik", t1, row_onehot, precision=hi)
    ok = row_ok[:, :, None, :, None] & col_ok[None, None, :, None, :]
    tab = jnp.where(ok, tab, NEG_INF).astype(_F32)
    return tab.reshape(rpb.shape[:2] + (3, NB_QROWS * GRID_W, NB_KROWS * GRID_W))


def _attn_b(qkv, qkv_ctx, bias_tabs):
    bsz, length, _ = qkv.shape
    n_ctx = qkv_ctx.shape[1]
    tq = NB_QROWS * GRID_W
    nt = length // tq
    nk = NB_KROWS * GRID_W

    def bias_map(b, h, t):
        return (h, jnp.where(t == 0, 0, jnp.where(t == nt - 1, 2, 1)), 0, 0)

    return pl.pallas_call(
        _attn_b_kernel,
        grid=(bsz, H_B, nt),
        in_specs=[
            pl.BlockSpec((1, tq, HEAD_DIM), lambda b, h, t: (b, t, QB0 + h)),
            pl.BlockSpec((1, length, HEAD_DIM), lambda b, h, t: (b, 0, KB0 + h)),
            pl.BlockSpec((1, length, HEAD_DIM), lambda b, h, t: (b, 0, VB0 + h)),
            pl.BlockSpec((1, n_ctx, HEAD_DIM), lambda b, h, t: (b, 0, KB0 + h)),
            pl.BlockSpec((1, n_ctx, HEAD_DIM), lambda b, h, t: (b, 0, VB0 + h)),
            pl.BlockSpec((1, 1, tq, nk), bias_map),
        ],
        out_specs=pl.BlockSpec((1, tq, HEAD_DIM), lambda b, h, t: (b, t, h)),
        out_shape=jax.ShapeDtypeStruct((bsz, length, H_B * HEAD_DIM), _BF16),
        compiler_params=_cparams("parallel", "parallel", "parallel"),
        name="attn_neighbourhood",
    )(qkv, qkv, qkv, qkv_ctx, qkv_ctx, bias_tabs)


def _sink_column(sink_ref, rows_per_head):
    sink = sink_ref[0]
    return jnp.concatenate(
        [jnp.broadcast_to(sink[g:g + 1, :1], (rows_per_head, 1)) for g in range(G_C)], axis=0)


def _attn_c_kernel(q0_ref, q1_ref, q2_ref, k_ref, v_ref, kc_ref, vc_ref, sink_ref, o_ref):
    t = pl.program_id(2)
    tq = q0_ref.shape[1]
    length = k_ref.shape[1]
    band = 3 * WINDOW
    kc = kc_ref[0]
    vc = vc_ref[0]
    sink_col = _sink_column(sink_ref, WINDOW)
    rel = (lax.broadcasted_iota(jnp.int32, (G_C * WINDOW, band), 0) % WINDOW
           - lax.broadcasted_iota(jnp.int32, (G_C * WINDOW, band), 1))
    for n in range(tq // WINDOW):
        i0 = t * tq + n * WINDOW
        start = pl.multiple_of(jnp.clip(i0 - WINDOW, 0, length - band), WINDOW)
        kb = k_ref[0, pl.ds(start, band), :]
        vb = v_ref[0, pl.ds(start, band), :]
        q3 = jnp.concatenate([r[0, n * WINDOW:(n + 1) * WINDOW, :] for r in (q0_ref, q1_ref, q2_ref)],
                             axis=0)
        keep = jnp.abs(rel + (i0 - start)) <= WINDOW
        o = _attend(q3, [(kb, vb, None, keep), (kc, vc, None, None)], sink_col)
        o_ref[0, n * WINDOW:(n + 1) * WINDOW, :] = jnp.concatenate(
            [o[g * WINDOW:(g + 1) * WINDOW] for g in range(G_C)], axis=1).astype(_BF16)


def _attn_c(qkv, qkv_ctx, sink_tab, tq):
    bsz, length, _ = qkv.shape
    n_ctx = qkv_ctx.shape[1]
    tq = min(tq, length)
    q_specs = [pl.BlockSpec((1, tq, HEAD_DIM), functools.partial(
        lambda b, h, t, g: (b, t, QC0 + h * G_C + g), g=g)) for g in range(G_C)]
    return pl.pallas_call(
        _attn_c_kernel,
        grid=(bsz, KV_C, length // tq),
        in_specs=q_specs + [
            pl.BlockSpec((1, length, HEAD_DIM), lambda b, h, t: (b, 0, KC0 + h)),
            pl.BlockSpec((1, length, HEAD_DIM), lambda b, h, t: (b, 0, VC0 + h)),
            pl.BlockSpec((1, n_ctx, HEAD_DIM), lambda b, h, t: (b, 0, KC0 + h)),
            pl.BlockSpec((1, n_ctx, HEAD_DIM), lambda b, h, t: (b, 0, VC0 + h)),
            pl.BlockSpec((1, G_C, HEAD_DIM), lambda b, h, t: (h, 0, 0)),
        ],
        out_specs=pl.BlockSpec((1, tq, G_C * HEAD_DIM), lambda b, h, t: (b, t, h)),
        out_shape=jax.ShapeDtypeStruct((bsz, length, H_C * HEAD_DIM), _BF16),
        compiler_params=_cparams("parallel", "parallel", "parallel"),
        name="attn_window",
    )(qkv, qkv, qkv, qkv, qkv, qkv_ctx, qkv_ctx, sink_tab)


def _attn_ctx_kernel(qkv_ref, sink_ref, ya_ref, yb_ref, yc_ref):
    n = qkv_ref.shape[1]

    def head(u):
        return qkv_ref[0, :, u * HEAD_DIM:(u + 1) * HEAD_DIM]

    for kv in range(KV_A):
        q = jnp.concatenate([head(QA0 + kv * G_A + g) for g in range(G_A)], axis=0)
        o = _attend(q, [(head(KA0 + kv), head(VA0 + kv), None, None)])
        for g in range(G_A):
            u = kv * G_A + g
            ya_ref[0, :, u * HEAD_DIM:(u + 1) * HEAD_DIM] = o[g * n:(g + 1) * n].astype(_BF16)
    for hb in range(H_B):
        o = _attend(head(QB0 + hb), [(head(KB0 + hb), head(VB0 + hb), None, None)])
        yb_ref[0, :, hb * HEAD_DIM:(hb + 1) * HEAD_DIM] = o.astype(_BF16)
    for kv in range(KV_C):
        q = jnp.concatenate([head(QC0 + kv * G_C + g) for g in range(G_C)], axis=0)
        sink = sink_ref[kv]
        sink_col = jnp.concatenate(
            [jnp.broadcast_to(sink[g:g + 1, :1], (n, 1)) for g in range(G_C)], axis=0)
        o = _attend(q, [(head(KC0 + kv), head(VC0 + kv), None, None)], sink_col)
        for g in range(G_C):
            u = kv * G_C + g
            yc_ref[0, :, u * HEAD_DIM:(u + 1) * HEAD_DIM] = o[g * n:(g + 1) * n].astype(_BF16)


def _attn_ctx(qkv_ctx, sink_tab):
    bsz, n, d_in = qkv_ctx.shape
    widths = (H_A * HEAD_DIM, H_B * HEAD_DIM, H_C * HEAD_DIM)
    return pl.pallas_call(
        _attn_ctx_kernel,
        grid=(bsz,),
        in_specs=[pl.BlockSpec((1, n, d_in), lambda b: (b, 0, 0)),
                  pl.BlockSpec(sink_tab.shape, lambda b: (0, 0, 0))],
        out_specs=[pl.BlockSpec((1, n, w), lambda b: (b, 0, 0)) for w in widths],
        out_shape=[jax.ShapeDtypeStruct((bsz, n, w), _BF16) for w in widths],
        compiler_params=_cparams("parallel"),
        name="attn_ctx",
    )(qkv_ctx, sink_tab)


def _outproj_kernel(ya_ref, yb_ref, yc_ref, x_ref, mod_ref, g_ref, w_ref, xo_ref, h_ref):
    d = x_ref.shape[-1]
    acc = None
    r0 = 0
    for y_ref in (ya_ref, yb_ref, yc_ref):
        r1 = r0 + y_ref.shape[-1]
        part = jnp.dot(y_ref[0], w_ref[r0:r1, :], preferred_element_type=_F32)
        acc = part if acc is None else acc + part
        r0 = r1
    xn = x_ref[0] + mod_ref[0, :, 2 * d:3 * d] * acc
    xo_ref[0] = xn
    h_ref[0] = _norm_mod(xn, g_ref[...], mod_ref[0, :, 3 * d:4 * d],
                         mod_ref[0, :, 4 * d:5 * d]).astype(_BF16)


def _outproj(ya, yb, yc, x, mod, mod_row, gain, w_out, tm):
    bsz, length, d = x.shape
    tm = min(tm, length)
    if mod_row is None:
        mod_map = lambda b, i: (b, 0, 0)
    else:
        mod_map = lambda b, i: (mod_row, 0, 0)
    row = lambda b, i: (b, i, 0)
    return pl.pallas_call(
        _outproj_kernel,
        grid=(bsz, length // tm),
        in_specs=[pl.BlockSpec((1, tm, y.shape[-1]), row) for y in (ya, yb, yc)] + [
            pl.BlockSpec((1, tm, d), row),
            pl.BlockSpec((1, 1, mod.shape[-1]), mod_map),
            pl.BlockSpec((1, d), lambda b, i: (0, 0)),
            pl.BlockSpec(w_out.shape, lambda b, i: (0, 0), pipeline_mode=pl.Buffered(1)),
        ],
        out_specs=[pl.BlockSpec((1, tm, d), row), pl.BlockSpec((1, tm, d), row)],
        out_shape=[jax.ShapeDtypeStruct((bsz, length, d), _F32),
                   jax.ShapeDtypeStruct((bsz, length, d), _BF16)],
        compiler_params=_cparams("parallel", "parallel"),
        name="out_proj_residual",
    )(ya, yb, yc, x, mod, gain, w_out)


def _mlp_kernel(h_ref, x_ref, mod_ref, wu_ref, wd_ref, o_ref):
    k = pl.program_id(2)
    d = x_ref.shape[-1]
    u = jnp.dot(h_ref[0], wu_ref[...], preferred_element_type=_F32)
    act = jnp.square(jnp.maximum(u, 0.0)).astype(_BF16)
    y = jnp.dot(act, wd_ref[...], preferred_element_type=_F32)

    @pl.when(k == 0)
    def _():
        o_ref[0] = y

    @pl.when(k > 0)
    def _():
        o_ref[0] += y

    @pl.when(k == pl.num_programs(2) - 1)
    def _():
        o_ref[0] = x_ref[0] + mod_ref[0, :, 5 * d:6 * d] * o_ref[0]


def _mlp(h, x, mod, mod_row, w_up, w_down, tm, tf):
    bsz, length, d = x.shape
    d_ff = w_up.shape[1]
    tm = min(tm, length)
    tf = min(tf, d_ff)
    if mod_row is None:
        mod_map = lambda b, i, k: (b, 0, 0)
    else:
        mod_map = lambda b, i, k: (mod_row, 0, 0)
    row = lambda b, i, k: (b, i, 0)
    return pl.pallas_call(
        _mlp_kernel,
        grid=(bsz, length // tm, d_ff // tf),
        in_specs=[pl.BlockSpec((1, tm, d), row),
                  pl.BlockSpec((1, tm, d), row),
                  pl.BlockSpec((1, 1, mod.shape[-1]), mod_map),
                  pl.BlockSpec((d, tf), lambda b, i, k: (0, k)),
                  pl.BlockSpec((tf, d), lambda b, i, k: (k, 0))],
        out_specs=pl.BlockSpec((1, tm, d), row),
        out_shape=jax.ShapeDtypeStruct((bsz, length, d), _F32),
        compiler_params=_cparams("parallel", "parallel", "arbitrary"),
        name="mlp_sq_relu",
    )(h, x, mod, w_up, w_down)


def _rope_tables(length):
    t = np.arange(length)
    nf = HEAD_DIM // 4
    inv = ROPE_THETA ** (-jnp.arange(nf, dtype=_F32) / nf)
    ar = jnp.asarray(t // GRID_W, _F32)[:, None] * inv
    ac = jnp.asarray(t % GRID_W, _F32)[:, None] * inv
    ang = jnp.concatenate([ar, ar, ac, ac], axis=-1)
    cos, sin = jnp.cos(ang), jnp.sin(ang)
    first = (np.arange(HEAD_DIM) // nf) % 2 == 0
    return cos, jnp.where(first, -sin, 0.0), jnp.where(first, 0.0, sin)


def kernel(x, c, ctx, c_ctx, w_mod, b_mod, norm_attn, norm_mlp, w_in, q_norm, k_norm,
           rel_pos_bias, sink_logits, w_out, w_up, w_down):
    bsz, length, d = x.shape
    depth = w_mod.shape[0]
    assert length % (NB_QROWS * GRID_W) == 0 and length // GRID_W >= NB_KROWS
    assert w_in.shape[-1] == N_HEADS_IN * HEAD_DIM and w_out.shape[1] == N_HEADS_OUT * HEAD_DIM

    mod_rows = 8
    c_rows = jnp.zeros((mod_rows, d), _F32).at[:bsz].set(c).at[bsz].set(c_ctx)
    mod = _modulation(c_rows, w_mod, b_mod).reshape(depth, mod_rows, 1, N_MOD * d)

    w_in_b, w_out_b = w_in.astype(_BF16), w_out.astype(_BF16)
    w_up_b, w_down_b = w_up.astype(_BF16), w_down.astype(_BF16)
    rope_tabs = _rope_tables(length)
    bias_tabs = _nb_bias_tables(rel_pos_bias, length // GRID_W)
    sink_tabs = jnp.broadcast_to(
        sink_logits.reshape(depth, KV_C, G_C, 1), (depth, KV_C, G_C, HEAD_DIM)).astype(_F32)

    cs = ctx
    for l in range(depth):
        last = l == depth - 1
        g_attn = norm_attn[l].reshape(1, d)
        g_mlp = norm_mlp[l].reshape(1, d)
        qkv = _qkv_proj(x, mod[l], None, g_attn, w_in_b[l], q_norm[l], k_norm[l], rope_tabs, tm=256)
        qkv_c = _qkv_proj(cs, mod[l], bsz, g_attn, w_in_b[l], q_norm[l], k_norm[l], None, tm=256)
        ya = _attn_a(qkv, qkv_c, tq=256, tk=512)
        yb = _attn_b(qkv, qkv_c, bias_tabs[l])
        yc = _attn_c(qkv, qkv_c, sink_tabs[l], tq=512)
        x, h = _outproj(ya, yb, yc, x, mod[l], None, g_mlp, w_out_b[l], tm=512)
        x = _mlp(h, x, mod[l], None, w_up_b[l], w_down_b[l], tm=512, tf=1024)
        if not last:
            ya_c, yb_c, yc_c = _attn_ctx(qkv_c, sink_tabs[l])
            cs, hc = _outproj(ya_c, yb_c, yc_c, cs, mod[l], bsz, g_mlp, w_out_b[l], tm=256)
            cs = _mlp(hc, cs, mod[l], bsz, w_up_b[l], w_down_b[l], tm=256, tf=512)
    return x
```

```python
import functools

import numpy as np
import jax
import jax.numpy as jnp
from jax import lax
from jax.experimental import pallas as pl
from jax.experimental.pallas import tpu as pltpu

HEAD_DIM = 128
H_A, KV_A = 4, 2
H_B = 6
H_C, KV_C = 6, 2
G_A = H_A // KV_A
G_C = H_C // KV_C
GRID_W = 64
NA_ROWS = 8
NA_COLS = 16
WINDOW = 128
ROPE_THETA = 10000.0
EPS = 1e-6
N_MOD = 6
ATTN_SCALE = HEAD_DIM ** -0.5
NEG_INF = -1e30
LOG2_E = 1.4426950408889634
LOGIT2_SCALE = ATTN_SCALE * LOG2_E

_IN_QA, _IN_KA, _IN_VA = 0, H_A, H_A + KV_A
_IN_QB = _IN_VA + KV_A
_IN_KB, _IN_VB = _IN_QB + H_B, _IN_QB + 2 * H_B
_IN_QC = _IN_VB + H_B
_IN_KC, _IN_VC = _IN_QC + H_C, _IN_QC + H_C + KV_C
N_HEADS_IN = _IN_VC + KV_C
N_HEADS_OUT = H_A + H_B + H_C

QC0 = 0
QA0 = QC0 + H_C
KA0 = QA0 + H_A
VA0 = KA0 + KV_A
QB0 = VA0 + KV_A
KB0 = QB0 + H_B
VB0 = KB0 + H_B
KC0 = VB0 + H_B
VC0 = KC0 + KV_C
assert QA0 % G_A == 0 and QC0 % G_C == 0 and VC0 + KV_C == N_HEADS_IN

_REGIONS = (
    (_IN_QA, QA0, H_A, "q", 0, True), (_IN_KA, KA0, KV_A, "k", 0, True), (_IN_VA, VA0, KV_A, None, 0, False),
    (_IN_QB, QB0, H_B, "q", 1, False), (_IN_KB, KB0, H_B, "k", 1, False), (_IN_VB, VB0, H_B, None, 0, False),
    (_IN_QC, QC0, H_C, "q", 2, True), (_IN_KC, KC0, KV_C, "k", 2, True), (_IN_VC, VC0, KV_C, None, 0, False),
)

NB_QROWS = 4
NB_KROWS = NB_QROWS + NA_ROWS - 1
NB_HEADS_PER_STEP = 2
OUTPROJ_ROW_CHUNK = 256
VMEM_LIMIT_BYTES = 56 * 1024 * 1024

_BF16 = jnp.bfloat16
_F32 = jnp.float32
_NT = (((1,), (1,)), ((), ()))
_TN = (((0,), (0,)), ((), ()))


def _cparams(*sem):
    return pltpu.CompilerParams(dimension_semantics=sem, vmem_limit_bytes=VMEM_LIMIT_BYTES)


def _norm_mod(x, gain, shift, scale):
    y = x * lax.rsqrt(jnp.mean(x * x, axis=-1, keepdims=True) + EPS)
    return (y * gain) * (1.0 + scale) + shift


def _logits2_t(k, q):
    return lax.dot_general(k, q, _NT, preferred_element_type=_F32) * LOGIT2_SCALE


def _attend_logits(q, kvs):
    zs = []
    for k, _, bias2, keep in kvs:
        z = _logits2_t(k, q)
        if bias2 is not None:
            z = z + bias2
        if keep is not None:
            z = jnp.where(keep, z, NEG_INF)
        zs.append(z)
    return zs


def _attend_t(q, kvs, sink2_row=None, zs=None):
    if zs is None:
        zs = _attend_logits(q, kvs)
    m = zs[0].max(axis=0, keepdims=True)
    for z in zs[1:]:
        m = jnp.maximum(m, z.max(axis=0, keepdims=True))
    if sink2_row is not None:
        m = jnp.maximum(m, sink2_row)
    l = None
    acc = None
    for z, (_, v, _, _) in zip(zs, kvs):
        p = jnp.exp2(z - m)
        ls = p.sum(axis=0, keepdims=True)
        pv = lax.dot_general(v, p.astype(_BF16), _TN, preferred_element_type=_F32)
        l = ls if l is None else l + ls
        acc = pv if acc is None else acc + pv
    if sink2_row is not None:
        l = l + jnp.exp2(sink2_row - m)
    return acc / l


def _attend_units(units):
    outs = []
    zs_next = _attend_logits(*units[0][:2])
    for u, (q, kvs, sink2) in enumerate(units):
        zs = zs_next
        if u + 1 < len(units):
            zs_next = _attend_logits(*units[u + 1][:2])
        outs.append(_attend_t(q, kvs, sink2, zs))
    return outs


def _mod_kernel(c_ref, w_ref, b_ref, o_ref):
    c = c_ref[...]
    cond = c * jax.nn.sigmoid(c)
    o_ref[0] = jnp.dot(cond.astype(_BF16), w_ref[0].astype(_BF16),
                       preferred_element_type=_F32) + b_ref[0]


def _modulation(c_rows, w_mod, b_mod):
    depth, d, n = w_mod.shape
    tn = min(n, 512)
    rows = c_rows.shape[0]
    return pl.pallas_call(
        _mod_kernel,
        grid=(depth, n // tn),
        in_specs=[pl.BlockSpec((rows, d), lambda l, j: (0, 0)),
                  pl.BlockSpec((1, d, tn), lambda l, j: (l, 0, j)),
                  pl.BlockSpec((1, 1, tn), lambda l, j: (l, 0, j))],
        out_specs=pl.BlockSpec((1, rows, tn), lambda l, j: (l, 0, j)),
        out_shape=jax.ShapeDtypeStruct((depth, rows, n), _F32),
        compiler_params=_cparams("parallel", "parallel"),
        name="adaln_modulation",
    )(c_rows, w_mod, b_mod.reshape(depth, 1, n))


def _mod_spec(mod, layer, mod_row, n_grid):
    block = (1, 1, 1, mod.shape[-1])
    if n_grid == 2:
        return pl.BlockSpec(block, (lambda b, i: (layer, b, 0, 0)) if mod_row is None
                            else (lambda b, i: (layer, mod_row, 0, 0)))
    return pl.BlockSpec(block, (lambda b, i, k: (layer, b, 0, 0)) if mod_row is None
                        else (lambda b, i, k: (layer, mod_row, 0, 0)))


def _qkv_kernel(x_ref, mod_ref, g_ref, w_ref, qn_ref, kn_ref, *rest, rope):
    if rope:
        cos_ref, sa_ref, sb_ref, o_ref = rest
    else:
        (o_ref,) = rest
    d = x_ref.shape[-1]
    mod = mod_ref[0, 0]
    h = _norm_mod(x_ref[0], g_ref[0], mod[:, 0:d], mod[:, d:2 * d]).astype(_BF16)
    for c0, o0, nh, kind, row, rotary in _REGIONS:
        y = jnp.dot(h, w_ref[0, :, c0 * HEAD_DIM:(c0 + nh) * HEAD_DIM], preferred_element_type=_F32)
        for j in range(nh):
            yj = y[:, j * HEAD_DIM:(j + 1) * HEAD_DIM]
            if kind is not None:
                gain = (qn_ref if kind == "q" else kn_ref)[0, row:row + 1, :]
                yj = yj * lax.rsqrt(jnp.mean(yj * yj, axis=-1, keepdims=True) + EPS) * gain
                if rope and rotary:
                    yj = (yj * cos_ref[...]
                          + pltpu.roll(yj, HEAD_DIM - HEAD_DIM // 4, 1) * sa_ref[...]
                          + pltpu.roll(yj, HEAD_DIM // 4, 1) * sb_ref[...])
            o_ref[0, o0 + j] = yj.astype(_BF16)


def _qkv_proj(x, mod, layer, mod_row, gain, w_in, qn, kn, rope_tabs, tm):
    bsz, length, d = x.shape
    d_in = w_in.shape[-1]
    tm = min(tm, length)
    in_specs = [
        pl.BlockSpec((1, tm, d), lambda b, i: (b, i, 0)),
        _mod_spec(mod, layer, mod_row, 2),
        pl.BlockSpec((1, 1, d), lambda b, i: (layer, 0, 0)),
        pl.BlockSpec((1, d, d_in), lambda b, i: (layer, 0, 0), pipeline_mode=pl.Buffered(1)),
        pl.BlockSpec((1,) + qn.shape[1:], lambda b, i: (layer, 0, 0)),
        pl.BlockSpec((1,) + kn.shape[1:], lambda b, i: (layer, 0, 0)),
    ]
    args = [x, mod, gain, w_in, qn, kn]
    if rope_tabs is not None:
        in_specs += [pl.BlockSpec((tm, HEAD_DIM), lambda b, i: (i, 0))] * 3
        args += list(rope_tabs)
    return pl.pallas_call(
        functools.partial(_qkv_kernel, rope=rope_tabs is not None),
        grid=(bsz, length // tm),
        in_specs=in_specs,
        out_specs=pl.BlockSpec((1, N_HEADS_IN, tm, HEAD_DIM), lambda b, i: (b, 0, i, 0)),
        out_shape=jax.ShapeDtypeStruct((bsz, N_HEADS_IN, length, HEAD_DIM), _BF16),
        compiler_params=_cparams("parallel", "parallel"),
        name="qkv_proj_rope" if rope_tabs is not None else "qkv_proj_ctx",
    )(*args)


def _head_spec(rows, head0, heads=1):
    return pl.BlockSpec((1, heads, rows, HEAD_DIM), lambda b, h, t: (b, head0 // heads + h, 0, 0))


def _query_spec(rows, head0, heads):
    return pl.BlockSpec((1, heads, rows, HEAD_DIM), lambda b, h, t: (b, head0 // heads + h, t, 0))


def _attn_a_kernel(q_ref, k_ref, v_ref, kc_ref, vc_ref, o_ref, acc_sc, s0_sc, s1_sc, *, tk):
    tq = q_ref.shape[2]
    length = k_ref.shape[2]
    nq = G_A * tq
    q2 = q_ref[0].reshape(nq, HEAD_DIM)
    c = LOGIT2_SCALE

    def qk(kb):
        return lax.dot_general(kb, q2, _NT, preferred_element_type=_F32)

    def softmax_pv(s, vb, m_prev, l_prev):
        m_new = jnp.maximum(m_prev, s.max(axis=0, keepdims=True))
        alpha = jnp.exp2(c * (m_prev - m_new))
        p = jnp.exp2(c * s - c * m_new)
        l_new = alpha * l_prev + p.sum(axis=0, keepdims=True)
        pv = lax.dot_general(vb, p.astype(_BF16), _TN, preferred_element_type=_F32)
        acc_sc[...] = alpha * acc_sc[...] + pv
        return m_new, l_new

    def keys(ref, j):
        return ref[0, 0, pl.ds(pl.multiple_of(j * tk, tk), tk), :]

    n_blocks = length // tk
    acc_sc[...] = jnp.zeros_like(acc_sc)
    m0 = jnp.full((1, nq), NEG_INF, _F32)
    l0 = jnp.zeros((1, nq), _F32)
    s0_sc[...] = qk(keys(k_ref, 0))
    m, l = softmax_pv(qk(kc_ref[0, 0]), vc_ref[0, 0], m0, l0)

    def pair(j0, carry, last):
        s1_sc[...] = qk(keys(k_ref, j0 + 1))
        carry = softmax_pv(s0_sc[...], keys(v_ref, j0), *carry)
        if not last:
            s0_sc[...] = qk(keys(k_ref, j0 + 2))
        return softmax_pv(s1_sc[...], keys(v_ref, j0 + 1), *carry)

    m, l = lax.fori_loop(0, n_blocks // 2 - 1, lambda jj, carry: pair(2 * jj, carry, False), (m, l))
    m, l = pair(n_blocks - 2, (m, l), True)
    o = (acc_sc[...] / l).T
    o_ref[0] = jnp.concatenate([o[g * tq:(g + 1) * tq] for g in range(G_A)], axis=1).astype(_BF16)


def _attn_a(qkv, qkv_ctx, tq, tk):
    bsz, _, length, _ = qkv.shape
    n_ctx = qkv_ctx.shape[2]
    tq = min(tq, length)
    tk = min(tk, length // 2)
    assert length % (2 * tk) == 0
    return pl.pallas_call(
        functools.partial(_attn_a_kernel, tk=tk),
        grid=(bsz, KV_A, length // tq),
        in_specs=[_query_spec(tq, QA0, G_A), _head_spec(length, KA0), _head_spec(length, VA0),
                  _head_spec(n_ctx, KA0), _head_spec(n_ctx, VA0)],
        out_specs=pl.BlockSpec((1, tq, G_A * HEAD_DIM), lambda b, h, i: (b, i, h)),
        out_shape=jax.ShapeDtypeStruct((bsz, length, H_A * HEAD_DIM), _BF16),
        scratch_shapes=[pltpu.VMEM((HEAD_DIM, G_A * tq), _F32),
                        pltpu.VMEM((tk, G_A * tq), _F32), pltpu.VMEM((tk, G_A * tq), _F32)],
        compiler_params=_cparams("parallel", "parallel", "parallel"),
        name="attn_global",
    )(qkv, qkv, qkv, qkv_ctx, qkv_ctx)


def _attn_b_kernel(q_ref, k_ref, v_ref, kc_ref, vc_ref, bias_ref, o_ref):
    n_rows = k_ref.shape[2] // GRID_W
    tq = NB_QROWS * GRID_W
    nk = NB_KROWS * GRID_W
    tiles = q_ref.shape[2] // tq
    n_tiles = pl.num_programs(2) * tiles
    units = []
    for i in range(tiles):
        t = pl.program_id(2) * tiles + i
        ks = jnp.clip(t * NB_QROWS - NA_ROWS // 2, 0, n_rows - NB_KROWS)
        off = pl.multiple_of(ks * GRID_W, GRID_W)
        variant = jnp.where(t == 0, 0, jnp.where(t == n_tiles - 1, 2, 1))
        for hh in range(NB_HEADS_PER_STEP):
            kvs = [(k_ref[0, hh, pl.ds(off, nk), :], v_ref[0, hh, pl.ds(off, nk), :],
                    bias_ref[0, hh, variant], None),
                   (kc_ref[0, hh], vc_ref[0, hh], None, None)]
            units.append((q_ref[0, hh, i * tq:(i + 1) * tq, :], kvs, None))
    outs = _attend_units(units)
    for i in range(tiles):
        heads = outs[i * NB_HEADS_PER_STEP:(i + 1) * NB_HEADS_PER_STEP]
        o_ref[0, i * tq:(i + 1) * tq, :] = jnp.concatenate([o.T for o in heads], axis=1).astype(_BF16)


def _nb_bias_tables(rpb, n_rows):
    kh = NA_ROWS
    c = np.arange(GRID_W)[:, None, None]
    j = np.arange(2 * NA_COLS - 1)[None, :, None]
    ck = np.arange(GRID_W)[None, None, :]
    col_onehot = (j == ck - c + (NA_COLS - 1)).astype(np.float32)
    cs = np.clip(c - NA_COLS // 2, 0, GRID_W - NA_COLS)
    col_ok = ((ck >= cs) & (ck < cs + NA_COLS))[:, 0, :]
    a = np.arange(NB_QROWS)[:, None, None]
    i = np.arange(NB_KROWS)[None, :, None]
    r = np.arange(2 * NA_ROWS - 1)[None, None, :]
    row_onehot, row_ok = [], []
    for r0, ks in ((0, 0), (NB_QROWS, 0), (n_rows - NB_QROWS, n_rows - NB_KROWS)):
        rq = r0 + a
        rk = ks + i
        rs = np.clip(rq - kh // 2, 0, n_rows - kh)
        row_ok.append(((rk >= rs) & (rk < rs + kh))[:, :, 0])
        row_onehot.append((r == rk - rq + (NA_ROWS - 1)).astype(np.float32))
    row_onehot, row_ok = np.stack(row_onehot), np.stack(row_ok)
    hi = lax.Precision.HIGHEST
    t1 = jnp.einsum("lhrj,cjk->lhrck", rpb * LOG2_E, col_onehot, precision=hi)
    tab = jnp.einsum("lhrck,vair->lhvikac", t1, row_onehot, precision=hi)
    ok = (row_ok.transpose(0, 2, 1)[:, :, None, :, None]
          & col_ok.T[None, None, :, None, :])
    tab = jnp.where(ok, tab, NEG_INF).astype(_F32)
    return tab.reshape(rpb.shape[:2] + (3, NB_KROWS * GRID_W, NB_QROWS * GRID_W))


def _attn_b(qkv, qkv_ctx, bias_tabs, layer, tiles):
    bsz, _, length, _ = qkv.shape
    n_ctx = qkv_ctx.shape[2]
    tq = tiles * NB_QROWS * GRID_W
    assert length % tq == 0
    hps = NB_HEADS_PER_STEP
    return pl.pallas_call(
        _attn_b_kernel,
        grid=(bsz, H_B // hps, length // tq),
        in_specs=[_query_spec(tq, QB0, hps), _head_spec(length, KB0, hps), _head_spec(length, VB0, hps),
                  _head_spec(n_ctx, KB0, hps), _head_spec(n_ctx, VB0, hps),
                  pl.BlockSpec((1, hps) + bias_tabs.shape[2:], lambda b, h, t: (layer, h, 0, 0, 0))],
        out_specs=pl.BlockSpec((1, tq, hps * HEAD_DIM), lambda b, h, t: (b, t, h)),
        out_shape=jax.ShapeDtypeStruct((bsz, length, H_B * HEAD_DIM), _BF16),
        compiler_params=_cparams("parallel", "parallel", "parallel"),
        name="attn_neighbourhood",
    )(qkv, qkv, qkv, qkv_ctx, qkv_ctx, bias_tabs)


def _sink2_row(sink):
    return jnp.concatenate([sink[g:g + 1, :] for g in range(G_C)], axis=1) * LOG2_E


def _attn_c_kernel(q_ref, k_ref, v_ref, kc_ref, vc_ref, sink_ref, o_ref):
    t = pl.program_id(2)
    tq = q_ref.shape[2]
    length = k_ref.shape[2]
    band = 3 * WINDOW
    nq = G_C * WINDOW
    kc = kc_ref[0, 0]
    vc = vc_ref[0, 0]
    sink2 = _sink2_row(sink_ref[0, 0])
    rel = (lax.broadcasted_iota(jnp.int32, (band, nq), 1) % WINDOW
           - lax.broadcasted_iota(jnp.int32, (band, nq), 0))
    units = []
    for n in range(tq // WINDOW):
        i0 = t * tq + n * WINDOW
        start = pl.multiple_of(jnp.clip(i0 - WINDOW, 0, length - band), WINDOW)
        kb = k_ref[0, 0, pl.ds(start, band), :]
        vb = v_ref[0, 0, pl.ds(start, band), :]
        q3 = q_ref[0, :, n * WINDOW:(n + 1) * WINDOW, :].reshape(nq, HEAD_DIM)
        keep = jnp.abs(rel + (i0 - start)) <= WINDOW
        units.append((q3, [(kb, vb, None, keep), (kc, vc, None, None)], sink2))
    for n, o_t in enumerate(_attend_units(units)):
        o = o_t.T
        o_ref[0, n * WINDOW:(n + 1) * WINDOW, :] = jnp.concatenate(
            [o[g * WINDOW:(g + 1) * WINDOW] for g in range(G_C)], axis=1).astype(_BF16)


def _attn_c(qkv, qkv_ctx, sink_tabs, layer, tq):
    bsz, _, length, _ = qkv.shape
    n_ctx = qkv_ctx.shape[2]
    tq = min(tq, length)
    return pl.pallas_call(
        _attn_c_kernel,
        grid=(bsz, KV_C, length // tq),
        in_specs=[_query_spec(tq, QC0, G_C), _head_spec(length, KC0), _head_spec(length, VC0),
                  _head_spec(n_ctx, KC0), _head_spec(n_ctx, VC0),
                  pl.BlockSpec((1, 1, G_C, HEAD_DIM), lambda b, h, t: (layer, h, 0, 0))],
        out_specs=pl.BlockSpec((1, tq, G_C * HEAD_DIM), lambda b, h, t: (b, t, h)),
        out_shape=jax.ShapeDtypeStruct((bsz, length, H_C * HEAD_DIM), _BF16),
        compiler_params=_cparams("parallel", "parallel", "parallel"),
        name="attn_window",
    )(qkv, qkv, qkv, qkv_ctx, qkv_ctx, sink_tabs)


def _attn_ctx_kernel(qkv_ref, sink_ref, ya_ref, yb_ref, yc_ref):
    n = qkv_ref.shape[2]

    def head(u):
        return qkv_ref[0, u]

    def heads(u0, g):
        return qkv_ref[0, u0:u0 + g].reshape(g * n, HEAD_DIM)

    for kv in range(KV_A):
        o = _attend_t(heads(QA0 + kv * G_A, G_A), [(head(KA0 + kv), head(VA0 + kv), None, None)]).T
        for g in range(G_A):
            u = kv * G_A + g
            ya_ref[0, :, u * HEAD_DIM:(u + 1) * HEAD_DIM] = o[g * n:(g + 1) * n].astype(_BF16)
    for hb in range(H_B):
        o = _attend_t(head(QB0 + hb), [(head(KB0 + hb), head(VB0 + hb), None, None)]).T
        yb_ref[0, :, hb * HEAD_DIM:(hb + 1) * HEAD_DIM] = o.astype(_BF16)
    for kv in range(KV_C):
        sink = sink_ref[0, kv]
        sink2 = jnp.concatenate(
            [jnp.broadcast_to(sink[g:g + 1, :1], (1, n)) for g in range(G_C)], axis=1) * LOG2_E
        o = _attend_t(heads(QC0 + kv * G_C, G_C), [(head(KC0 + kv), head(VC0 + kv), None, None)], sink2).T
        for g in range(G_C):
            u = kv * G_C + g
            yc_ref[0, :, u * HEAD_DIM:(u + 1) * HEAD_DIM] = o[g * n:(g + 1) * n].astype(_BF16)


def _attn_ctx(qkv_ctx, sink_tabs, layer):
    bsz, n_heads, n, _ = qkv_ctx.shape
    widths = (H_A * HEAD_DIM, H_B * HEAD_DIM, H_C * HEAD_DIM)
    return pl.pallas_call(
        _attn_ctx_kernel,
        grid=(bsz,),
        in_specs=[pl.BlockSpec((1, n_heads, n, HEAD_DIM), lambda b: (b, 0, 0, 0)),
                  pl.BlockSpec((1,) + sink_tabs.shape[1:], lambda b: (layer, 0, 0, 0))],
        out_specs=[pl.BlockSpec((1, n, w), lambda b: (b, 0, 0)) for w in widths],
        out_shape=[jax.ShapeDtypeStruct((bsz, n, w), _BF16) for w in widths],
        compiler_params=_cparams("parallel"),
        name="attn_ctx",
    )(qkv_ctx, sink_tabs)


def _outproj_kernel(ya_ref, yb_ref, yc_ref, x_ref, mod_ref, g_ref, w_ref, xo_ref, h_ref):
    tm, d = x_ref.shape[1:]
    mod = mod_ref[0, 0]
    chunks = [slice(m0, min(m0 + OUTPROJ_ROW_CHUNK, tm)) for m0 in range(0, tm, OUTPROJ_ROW_CHUNK)]

    def project(rows):
        acc = None
        r0 = 0
        for y_ref in (ya_ref, yb_ref, yc_ref):
            r1 = r0 + y_ref.shape[-1]
            part = jnp.dot(y_ref[0, rows, :], w_ref[0, r0:r1, :], preferred_element_type=_F32)
            acc = part if acc is None else acc + part
            r0 = r1
        return acc

    acc_next = project(chunks[0])
    for i, rows in enumerate(chunks):
        acc = acc_next
        if i + 1 < len(chunks):
            acc_next = project(chunks[i + 1])
        xn = x_ref[0, rows, :] + mod[:, 2 * d:3 * d] * acc
        xo_ref[0, rows, :] = xn
        h_ref[0, rows, :] = _norm_mod(xn, g_ref[0], mod[:, 3 * d:4 * d],
                                      mod[:, 4 * d:5 * d]).astype(_BF16)


def _outproj(ya, yb, yc, x, mod, layer, mod_row, gain, w_out, tm):
    bsz, length, d = x.shape
    tm = min(tm, length)
    row = lambda b, i: (b, i, 0)
    return pl.pallas_call(
        _outproj_kernel,
        grid=(bsz, length // tm),
        in_specs=[pl.BlockSpec((1, tm, y.shape[-1]), row) for y in (ya, yb, yc)] + [
            pl.BlockSpec((1, tm, d), row),
            _mod_spec(mod, layer, mod_row, 2),
            pl.BlockSpec((1, 1, d), lambda b, i: (layer, 0, 0)),
            pl.BlockSpec((1,) + w_out.shape[1:], lambda b, i: (layer, 0, 0), pipeline_mode=pl.Buffered(1)),
        ],
        out_specs=[pl.BlockSpec((1, tm, d), row), pl.BlockSpec((1, tm, d), row)],
        out_shape=[jax.ShapeDtypeStruct((bsz, length, d), _F32),
                   jax.ShapeDtypeStruct((bsz, length, d), _BF16)],
        compiler_params=_cparams("parallel", "parallel"),
        name="out_proj_residual",
    )(ya, yb, yc, x, mod, gain, w_out)


def _mlp_kernel(h_ref, x_ref, mod_ref, wu_ref, wd_ref, o_ref):
    k = pl.program_id(2)
    d = x_ref.shape[-1]
    @pl.when(k == 0)
    def _():
        o_ref[...] = jnp.zeros_like(o_ref)

    u = jnp.dot(h_ref[0], wu_ref[0], preferred_element_type=_F32)
    act = jnp.square(jnp.maximum(u, 0.0)).astype(_BF16)
    o_ref[0] += jnp.dot(act, wd_ref[0], preferred_element_type=_F32)

    @pl.when(k == pl.num_programs(2) - 1)
    def _():
        o_ref[0] = x_ref[0] + mod_ref[0, 0, :, 5 * d:6 * d] * o_ref[0]


def _mlp(h, x, mod, layer, mod_row, w_up, w_down, tm, tf):
    bsz, length, d = x.shape
    d_ff = w_up.shape[-1]
    tm = min(tm, length)
    tf = min(tf, d_ff)
    row = lambda b, i, k: (b, i, 0)
    return pl.pallas_call(
        _mlp_kernel,
        grid=(bsz, length // tm, d_ff // tf),
        in_specs=[pl.BlockSpec((1, tm, d), row),
                  pl.BlockSpec((1, tm, d), row),
                  _mod_spec(mod, layer, mod_row, 3),
                  pl.BlockSpec((1, d, tf), lambda b, i, k: (layer, 0, k)),
                  pl.BlockSpec((1, tf, d), lambda b, i, k: (layer, k, 0))],
        out_specs=pl.BlockSpec((1, tm, d), row),
        out_shape=jax.ShapeDtypeStruct((bsz, length, d), _F32),
        compiler_params=_cparams("parallel", "parallel", "arbitrary"),
        name="mlp_sq_relu",
    )(h, x, mod, w_up, w_down)


def _rope_tables(length):
    t = np.arange(length)
    nf = HEAD_DIM // 4
    inv = ROPE_THETA ** (-jnp.arange(nf, dtype=_F32) / nf)
    ar = jnp.asarray(t // GRID_W, _F32)[:, None] * inv
    ac = jnp.asarray(t % GRID_W, _F32)[:, None] * inv
    ang = jnp.concatenate([ar, ar, ac, ac], axis=-1)
    cos, sin = jnp.cos(ang), jnp.sin(ang)
    first = (np.arange(HEAD_DIM) // nf) % 2 == 0
    return cos, jnp.where(first, -sin, 0.0), jnp.where(first, 0.0, sin)


def kernel(x, c, ctx, c_ctx, w_mod, b_mod, norm_attn, norm_mlp, w_in, q_norm, k_norm,
           rel_pos_bias, sink_logits, w_out, w_up, w_down):
    bsz, length, d = x.shape
    depth = w_mod.shape[0]
    assert length % (NB_QROWS * GRID_W) == 0 and length // GRID_W >= NB_KROWS
    assert w_in.shape[-1] == N_HEADS_IN * HEAD_DIM and w_out.shape[1] == N_HEADS_OUT * HEAD_DIM

    mod_rows = 8
    c_rows = jnp.zeros((mod_rows, d), _F32).at[:bsz].set(c).at[bsz].set(c_ctx)
    mod = _modulation(c_rows, w_mod, b_mod).reshape(depth, mod_rows, 1, N_MOD * d)

    w_in_b, w_out_b = w_in.astype(_BF16), w_out.astype(_BF16)
    w_up_b, w_down_b = w_up.astype(_BF16), w_down.astype(_BF16)
    g_attn = norm_attn.reshape(depth, 1, d)
    g_mlp = norm_mlp.reshape(depth, 1, d)
    rope_tabs = _rope_tables(length)
    bias_tabs = _nb_bias_tables(rel_pos_bias, length // GRID_W)
    sink_tabs = jnp.broadcast_to(
        sink_logits.reshape(depth, KV_C, G_C, 1), (depth, KV_C, G_C, HEAD_DIM)).astype(_F32)

    cs = ctx
    for l in range(depth):
        last = l == depth - 1
        qkv = _qkv_proj(x, mod, l, None, g_attn, w_in_b, q_norm, k_norm, rope_tabs, tm=256)
        qkv_c = _qkv_proj(cs, mod, l, bsz, g_attn, w_in_b, q_norm, k_norm, None, tm=256)
        ya = _attn_a(qkv, qkv_c, tq=256, tk=512)
        yb = _attn_b(qkv, qkv_c, bias_tabs, l, tiles=4)
        yc = _attn_c(qkv, qkv_c, sink_tabs, l, tq=512)
        x, h = _outproj(ya, yb, yc, x, mod, l, None, g_mlp, w_out_b, tm=512)
        x = _mlp(h, x, mod, l, None, w_up_b, w_down_b, tm=512, tf=1024)
        if not last:
            ya_c, yb_c, yc_c = _attn_ctx(qkv_c, sink_tabs, l)
            cs, hc = _outproj(ya_c, yb_c, yc_c, cs, mod, l, bsz, g_mlp, w_out_b, tm=256)
            cs = _mlp(hc, cs, mod, l, bsz, w_up_b, w_down_b, tm=256, tf=1024)
    return x
```

```python
import functools

import numpy as np
import jax
import jax.numpy as jnp
from jax import lax
from jax.experimental import pallas as pl
from jax.experimental.pallas import tpu as pltpu

HEAD_DIM = 128
H_A, KV_A = 4, 2
H_B = 6
H_C, KV_C = 6, 2
G_A = H_A // KV_A
G_C = H_C // KV_C
GRID_W = 64
NA_ROWS = 8
NA_COLS = 16
WINDOW = 128
ROPE_THETA = 10000.0
EPS = 1e-6
N_MOD = 6
ATTN_SCALE = HEAD_DIM ** -0.5
NEG_INF = -1e30
LOG2_E = 1.4426950408889634
LOGIT2_SCALE = ATTN_SCALE * LOG2_E

_IN_QA, _IN_KA, _IN_VA = 0, H_A, H_A + KV_A
_IN_QB = _IN_VA + KV_A
_IN_KB, _IN_VB = _IN_QB + H_B, _IN_QB + 2 * H_B
_IN_QC = _IN_VB + H_B
_IN_KC, _IN_VC = _IN_QC + H_C, _IN_QC + H_C + KV_C
N_HEADS_IN = _IN_VC + KV_C
N_HEADS_OUT = H_A + H_B + H_C

QC0 = 0
QA0 = QC0 + H_C
KA0 = QA0 + H_A
VA0 = KA0 + KV_A
QB0 = VA0 + KV_A
KB0 = QB0 + H_B
VB0 = KB0 + H_B
KC0 = VB0 + H_B
VC0 = KC0 + KV_C
assert QA0 % G_A == 0 and QC0 % G_C == 0 and VC0 + KV_C == N_HEADS_IN

_REGIONS = (
    (_IN_QA, QA0, H_A, "q", 0, True), (_IN_KA, KA0, KV_A, "k", 0, True), (_IN_VA, VA0, KV_A, None, 0, False),
    (_IN_QB, QB0, H_B, "q", 1, False), (_IN_KB, KB0, H_B, "k", 1, False), (_IN_VB, VB0, H_B, None, 0, False),
    (_IN_QC, QC0, H_C, "q", 2, True), (_IN_KC, KC0, KV_C, "k", 2, True), (_IN_VC, VC0, KV_C, None, 0, False),
)

NB_QROWS = 4
NB_KROWS = NB_QROWS + NA_ROWS - 1
NB_HEADS_PER_STEP = 2
OUTPROJ_ROW_CHUNK = 256
VMEM_LIMIT_BYTES = 56 * 1024 * 1024

_BF16 = jnp.bfloat16
_F32 = jnp.float32
_NT = (((1,), (1,)), ((), ()))
_TN = (((0,), (0,)), ((), ()))


def _cparams(*sem):
    return pltpu.CompilerParams(dimension_semantics=sem, vmem_limit_bytes=VMEM_LIMIT_BYTES)


def _norm_mod(x, gain, shift, scale):
    y = x * lax.rsqrt(jnp.mean(x * x, axis=-1, keepdims=True) + EPS)
    return (y * gain) * (1.0 + scale) + shift


def _logits2_t(k, q):
    return lax.dot_general(k, q, _NT, preferred_element_type=_F32)


def _attend_logits(q, kvs):
    zs = []
    for k, _, bias2, keep in kvs:
        z = _logits2_t(k, q)
        if bias2 is not None:
            z = z + bias2
        if keep is not None:
            z = jnp.where(keep, z, NEG_INF)
        zs.append(z)
    return zs


def _attend_t(q, kvs, sink2_row=None, zs=None):
    if zs is None:
        zs = _attend_logits(q, kvs)
    m = zs[0].max(axis=0, keepdims=True)
    for z in zs[1:]:
        m = jnp.maximum(m, z.max(axis=0, keepdims=True))
    if sink2_row is not None:
        m = jnp.maximum(m, sink2_row)
    l = None
    acc = None
    for z, (_, v, _, _) in zip(zs, kvs):
        p = jnp.exp2(z - m)
        ls = p.sum(axis=0, keepdims=True)
        pv = lax.dot_general(v, p.astype(_BF16), _TN, preferred_element_type=_F32)
        l = ls if l is None else l + ls
        acc = pv if acc is None else acc + pv
    if sink2_row is not None:
        l = l + jnp.exp2(sink2_row - m)
    return acc / l


def _attend_units(units):
    outs = []
    zs_next = _attend_logits(*units[0][:2])
    for u, (q, kvs, sink2) in enumerate(units):
        zs = zs_next
        if u + 1 < len(units):
            zs_next = _attend_logits(*units[u + 1][:2])
        outs.append(_attend_t(q, kvs, sink2, zs))
    return outs


def _mod_kernel(c_ref, w_ref, b_ref, o_ref):
    c = c_ref[...]
    cond = c * jax.nn.sigmoid(c)
    o_ref[0] = jnp.dot(cond.astype(_BF16), w_ref[0].astype(_BF16),
                       preferred_element_type=_F32) + b_ref[0]


def _modulation(c_rows, w_mod, b_mod):
    depth, d, n = w_mod.shape
    tn = next(t for t in (1024, 512, 256, 128, n) if n % t == 0)
    rows = c_rows.shape[0]
    return pl.pallas_call(
        _mod_kernel,
        grid=(depth, n // tn),
        in_specs=[pl.BlockSpec((rows, d), lambda l, j: (0, 0)),
                  pl.BlockSpec((1, d, tn), lambda l, j: (l, 0, j)),
                  pl.BlockSpec((1, 1, tn), lambda l, j: (l, 0, j))],
        out_specs=pl.BlockSpec((1, rows, tn), lambda l, j: (l, 0, j)),
        out_shape=jax.ShapeDtypeStruct((depth, rows, n), _F32),
        compiler_params=_cparams("parallel", "parallel"),
        name="adaln_modulation",
    )(c_rows, w_mod, b_mod.reshape(depth, 1, n))


def _mod_spec(mod, layer, mod_row, n_grid):
    block = (1, 1, 1, mod.shape[-1])
    if n_grid == 2:
        return pl.BlockSpec(block, (lambda b, i: (layer, b, 0, 0)) if mod_row is None
                            else (lambda b, i: (layer, mod_row, 0, 0)))
    return pl.BlockSpec(block, (lambda b, i, k: (layer, b, 0, 0)) if mod_row is None
                        else (lambda b, i, k: (layer, mod_row, 0, 0)))


def _qkv_kernel(x_ref, mod_ref, g_ref, w_ref, qn_ref, kn_ref, *rest, rope):
    if rope:
        cos_ref, sa_ref, sb_ref, o_ref, vt_ref = rest
    else:
        o_ref, vt_ref = rest
    d = x_ref.shape[-1]
    mod = mod_ref[0, 0]
    h = _norm_mod(x_ref[0], g_ref[0], mod[:, 0:d], mod[:, d:2 * d]).astype(_BF16)
    for c0, o0, nh, kind, row, rotary in _REGIONS:
        y = jnp.dot(h, w_ref[0, :, c0 * HEAD_DIM:(c0 + nh) * HEAD_DIM], preferred_element_type=_F32)
        for j in range(nh):
            yj = y[:, j * HEAD_DIM:(j + 1) * HEAD_DIM]
            if kind is not None:
                gain = (qn_ref if kind == "q" else kn_ref)[0, row:row + 1, :]
                yj = yj * lax.rsqrt(jnp.mean(yj * yj, axis=-1, keepdims=True) + EPS) * gain
                if rope and rotary:
                    yj = (yj * cos_ref[...]
                          + pltpu.roll(yj, HEAD_DIM - HEAD_DIM // 4, 1) * sa_ref[...]
                          + pltpu.roll(yj, HEAD_DIM // 4, 1) * sb_ref[...])
                if kind == "q":
                    yj = yj * LOGIT2_SCALE
            o_ref[0, o0 + j] = yj.astype(_BF16)
            if o0 == VA0:
                vt_ref[0, j] = yj.T.astype(_BF16)


def _qkv_proj(x, mod, layer, mod_row, gain, w_in, qn, kn, rope_tabs, tm):
    bsz, length, d = x.shape
    d_in = w_in.shape[-1]
    tm = min(tm, length)
    in_specs = [
        pl.BlockSpec((1, tm, d), lambda b, i: (b, i, 0)),
        _mod_spec(mod, layer, mod_row, 2),
        pl.BlockSpec((1, 1, d), lambda b, i: (layer, 0, 0)),
        pl.BlockSpec((1, d, d_in), lambda b, i: (layer, 0, 0), pipeline_mode=pl.Buffered(1)),
        pl.BlockSpec((1,) + qn.shape[1:], lambda b, i: (layer, 0, 0)),
        pl.BlockSpec((1,) + kn.shape[1:], lambda b, i: (layer, 0, 0)),
    ]
    args = [x, mod, gain, w_in, qn, kn]
    if rope_tabs is not None:
        in_specs += [pl.BlockSpec((tm, HEAD_DIM), lambda b, i: (i, 0))] * 3
        args += list(rope_tabs)
    return pl.pallas_call(
        functools.partial(_qkv_kernel, rope=rope_tabs is not None),
        grid=(bsz, length // tm),
        in_specs=in_specs,
        out_specs=[pl.BlockSpec((1, N_HEADS_IN, tm, HEAD_DIM), lambda b, i: (b, 0, i, 0)),
                   pl.BlockSpec((1, KV_A, HEAD_DIM, tm), lambda b, i: (b, 0, 0, i))],
        out_shape=[jax.ShapeDtypeStruct((bsz, N_HEADS_IN, length, HEAD_DIM), _BF16),
                   jax.ShapeDtypeStruct((bsz, KV_A, HEAD_DIM, length), _BF16)],
        compiler_params=_cparams("parallel", "parallel"),
        name="qkv_proj_rope" if rope_tabs is not None else "qkv_proj_ctx",
    )(*args)


def _head_spec(rows, head0, heads=1):
    return pl.BlockSpec((1, heads, rows, HEAD_DIM), lambda b, h, t: (b, head0 // heads + h, 0, 0))


def _ctx_head_spec(rows, head0, heads=1):
    return pl.BlockSpec((1, heads, rows, HEAD_DIM), lambda b, h, t: (0, head0 // heads + h, b, 0))


def _query_spec(rows, head0, heads):
    return pl.BlockSpec((1, heads, rows, HEAD_DIM), lambda b, h, t: (b, head0 // heads + h, t, 0))


def _attn_a_kernel(q_ref, k_ref, vt_ref, kc_ref, vtc_ref, o_ref, acc_sc, s0_sc, s1_sc, *, tk):
    tq = q_ref.shape[2]
    length = k_ref.shape[2]
    nq = G_A * tq
    q2 = q_ref[0].reshape(nq, HEAD_DIM)

    def qk(kb):
        return _logits2_t(kb, q2)

    def softmax_pv(s, vb, m_prev, l_prev):
        m_new = jnp.maximum(m_prev, s.max(axis=0, keepdims=True))
        alpha = jnp.exp2(m_prev - m_new)
        p = jnp.exp2(s - m_new)
        l_new = alpha * l_prev + p.sum(axis=0, keepdims=True)
        pv = jnp.dot(vb, p.astype(_BF16), preferred_element_type=_F32)
        acc_sc[...] = alpha * acc_sc[...] + pv
        return m_new, l_new

    def keys(ref, j):
        return ref[0, 0, pl.ds(pl.multiple_of(j * tk, tk), tk), :]

    def values_t(j):
        return vt_ref[0, 0, :, pl.ds(pl.multiple_of(j * tk, tk), tk)]

    n_blocks = length // tk
    acc_sc[...] = jnp.zeros_like(acc_sc)
    m0 = jnp.full((1, nq), NEG_INF, _F32)
    l0 = jnp.zeros((1, nq), _F32)
    s0_sc[...] = qk(keys(k_ref, 0))
    m, l = softmax_pv(qk(kc_ref[0, 0]), vtc_ref[0, 0], m0, l0)

    def pair(j0, carry, last):
        s1_sc[...] = qk(keys(k_ref, j0 + 1))
        carry = softmax_pv(s0_sc[...], values_t(j0), *carry)
        if not last:
            s0_sc[...] = qk(keys(k_ref, j0 + 2))
        return softmax_pv(s1_sc[...], values_t(j0 + 1), *carry)

    m, l = lax.fori_loop(0, n_blocks // 2 - 1, lambda jj, carry: pair(2 * jj, carry, False), (m, l))
    m, l = pair(n_blocks - 2, (m, l), True)
    o = (acc_sc[...] / l).T
    o_ref[0] = jnp.concatenate([o[g * tq:(g + 1) * tq] for g in range(G_A)], axis=1).astype(_BF16)


def _attn_a(qkv, vt, qkv_ctx, vt_ctx, tq, tk):
    bsz, _, length, _ = qkv.shape
    n_ctx = qkv_ctx.shape[2] // bsz
    tq = min(tq, length)
    tk = min(tk, length // 2)
    assert length % (2 * tk) == 0
    return pl.pallas_call(
        functools.partial(_attn_a_kernel, tk=tk),
        grid=(bsz, KV_A, length // tq),
        in_specs=[_query_spec(tq, QA0, G_A), _head_spec(length, KA0),
                  pl.BlockSpec((1, 1, HEAD_DIM, length), lambda b, h, i: (b, h, 0, 0)),
                  _ctx_head_spec(n_ctx, KA0),
                  pl.BlockSpec((1, 1, HEAD_DIM, n_ctx), lambda b, h, i: (0, h, 0, b))],
        out_specs=pl.BlockSpec((1, tq, G_A * HEAD_DIM), lambda b, h, i: (b, i, h)),
        out_shape=jax.ShapeDtypeStruct((bsz, length, H_A * HEAD_DIM), _BF16),
        scratch_shapes=[pltpu.VMEM((HEAD_DIM, G_A * tq), _F32),
                        pltpu.VMEM((tk, G_A * tq), _F32), pltpu.VMEM((tk, G_A * tq), _F32)],
        compiler_params=_cparams("parallel", "parallel", "parallel"),
        name="attn_global",
    )(qkv, qkv, vt, qkv_ctx, vt_ctx)


def _attn_b_kernel(q_ref, k_ref, v_ref, kc_ref, vc_ref, bias_ref, o_ref):
    n_rows = k_ref.shape[2] // GRID_W
    tq = NB_QROWS * GRID_W
    nk = NB_KROWS * GRID_W
    tiles = q_ref.shape[2] // tq
    n_tiles = pl.num_programs(2) * tiles
    units = []
    for i in range(tiles):
        t = pl.program_id(2) * tiles + i
        ks = jnp.clip(t * NB_QROWS - NA_ROWS // 2, 0, n_rows - NB_KROWS)
        off = pl.multiple_of(ks * GRID_W, GRID_W)
        variant = jnp.where(t == 0, 0, jnp.where(t == n_tiles - 1, 2, 1))
        for hh in range(NB_HEADS_PER_STEP):
            kvs = [(k_ref[0, hh, pl.ds(off, nk), :], v_ref[0, hh, pl.ds(off, nk), :],
                    bias_ref[0, hh, variant], None),
                   (kc_ref[0, hh], vc_ref[0, hh], None, None)]
            units.append((q_ref[0, hh, i * tq:(i + 1) * tq, :], kvs, None))
    outs = _attend_units(units)
    for i in range(tiles):
        heads = outs[i * NB_HEADS_PER_STEP:(i + 1) * NB_HEADS_PER_STEP]
        o_ref[0, i * tq:(i + 1) * tq, :] = jnp.concatenate([o.T for o in heads], axis=1).astype(_BF16)


def _nb_bias_tables(rpb, n_rows):
    kh = NA_ROWS
    c = np.arange(GRID_W)[:, None, None]
    j = np.arange(2 * NA_COLS - 1)[None, :, None]
    ck = np.arange(GRID_W)[None, None, :]
    col_onehot = (j == ck - c + (NA_COLS - 1)).astype(np.float32)
    cs = np.clip(c - NA_COLS // 2, 0, GRID_W - NA_COLS)
    col_ok = ((ck >= cs) & (ck < cs + NA_COLS))[:, 0, :]
    blocks = jnp.einsum("lhrj,cjk->lhrkc", rpb * LOG2_E, col_onehot, precision=lax.Precision.HIGHEST)
    blocks = jnp.where(col_ok.T, blocks, NEG_INF).astype(_F32)
    masked = jnp.full(blocks.shape[:2] + (GRID_W, GRID_W), NEG_INF, _F32)
    tabs = []
    for r0, ks in ((0, 0), (NB_QROWS, 0), (n_rows - NB_QROWS, n_rows - NB_KROWS)):
        key_rows = []
        for i in range(NB_KROWS):
            row = []
            for a in range(NB_QROWS):
                rq, rk = r0 + a, ks + i
                rs = min(max(rq - kh // 2, 0), n_rows - kh)
                row.append(blocks[:, :, rk - rq + (NA_ROWS - 1)] if rs <= rk < rs + kh else masked)
            key_rows.append(jnp.concatenate(row, axis=-1))
        tabs.append(jnp.concatenate(key_rows, axis=-2))
    return jnp.stack(tabs, axis=2)


def _attn_b(qkv, qkv_ctx, bias_tabs, layer, tiles):
    bsz, _, length, _ = qkv.shape
    n_ctx = qkv_ctx.shape[2] // bsz
    tq = tiles * NB_QROWS * GRID_W
    assert length % tq == 0
    hps = NB_HEADS_PER_STEP
    return pl.pallas_call(
        _attn_b_kernel,
        grid=(bsz, H_B // hps, length // tq),
        in_specs=[_query_spec(tq, QB0, hps), _head_spec(length, KB0, hps), _head_spec(length, VB0, hps),
                  _ctx_head_spec(n_ctx, KB0, hps), _ctx_head_spec(n_ctx, VB0, hps),
                  pl.BlockSpec((1, hps) + bias_tabs.shape[2:], lambda b, h, t: (layer, h, 0, 0, 0))],
        out_specs=pl.BlockSpec((1, tq, hps * HEAD_DIM), lambda b, h, t: (b, t, h)),
        out_shape=jax.ShapeDtypeStruct((bsz, length, H_B * HEAD_DIM), _BF16),
        compiler_params=_cparams("parallel", "parallel", "parallel"),
        name="attn_neighbourhood",
    )(qkv, qkv, qkv, qkv_ctx, qkv_ctx, bias_tabs)


def _sink2_row(sink):
    return jnp.concatenate([sink[g:g + 1, :] for g in range(G_C)], axis=1) * LOG2_E


def _attn_c_kernel(q_ref, k_ref, v_ref, kc_ref, vc_ref, sink_ref, o_ref):
    t = pl.program_id(2)
    tq = q_ref.shape[2]
    length = k_ref.shape[2]
    band = 3 * WINDOW
    nq = G_C * WINDOW
    kc = kc_ref[0, 0]
    vc = vc_ref[0, 0]
    sink2 = _sink2_row(sink_ref[0, 0])
    rel = (lax.broadcasted_iota(jnp.int32, (band, nq), 1) % WINDOW
           - lax.broadcasted_iota(jnp.int32, (band, nq), 0))
    units = []
    for n in range(tq // WINDOW):
        i0 = t * tq + n * WINDOW
        start = pl.multiple_of(jnp.clip(i0 - WINDOW, 0, length - band), WINDOW)
        kb = k_ref[0, 0, pl.ds(start, band), :]
        vb = v_ref[0, 0, pl.ds(start, band), :]
        q3 = q_ref[0, :, n * WINDOW:(n + 1) * WINDOW, :].reshape(nq, HEAD_DIM)
        keep = jnp.abs(rel + (i0 - start)) <= WINDOW
        units.append((q3, [(kb, vb, None, keep), (kc, vc, None, None)], sink2))
    for n, o_t in enumerate(_attend_units(units)):
        o = o_t.T
        o_ref[0, n * WINDOW:(n + 1) * WINDOW, :] = jnp.concatenate(
            [o[g * WINDOW:(g + 1) * WINDOW] for g in range(G_C)], axis=1).astype(_BF16)


def _attn_c(qkv, qkv_ctx, sink_tabs, layer, tq):
    bsz, _, length, _ = qkv.shape
    n_ctx = qkv_ctx.shape[2] // bsz
    tq = min(tq, length)
    return pl.pallas_call(
        _attn_c_kernel,
        grid=(bsz, KV_C, length // tq),
        in_specs=[_query_spec(tq, QC0, G_C), _head_spec(length, KC0), _head_spec(length, VC0),
                  _ctx_head_spec(n_ctx, KC0), _ctx_head_spec(n_ctx, VC0),
                  pl.BlockSpec((1, 1, G_C, HEAD_DIM), lambda b, h, t: (layer, h, 0, 0))],
        out_specs=pl.BlockSpec((1, tq, G_C * HEAD_DIM), lambda b, h, t: (b, t, h)),
        out_shape=jax.ShapeDtypeStruct((bsz, length, H_C * HEAD_DIM), _BF16),
        compiler_params=_cparams("parallel", "parallel", "parallel"),
        name="attn_window",
    )(qkv, qkv, qkv, qkv_ctx, qkv_ctx, sink_tabs)


def _attn_ctx_kernel(qkv_ref, sink_ref, ya_ref, yb_ref, yc_ref):
    n = qkv_ref.shape[2]

    def head(u):
        return qkv_ref[0, u]

    def heads(u0, g):
        return qkv_ref[0, u0:u0 + g].reshape(g * n, HEAD_DIM)

    for kv in range(KV_A):
        o = _attend_t(heads(QA0 + kv * G_A, G_A), [(head(KA0 + kv), head(VA0 + kv), None, None)]).T
        for g in range(G_A):
            u = kv * G_A + g
            ya_ref[0, :, u * HEAD_DIM:(u + 1) * HEAD_DIM] = o[g * n:(g + 1) * n].astype(_BF16)
    for hb in range(H_B):
        o = _attend_t(head(QB0 + hb), [(head(KB0 + hb), head(VB0 + hb), None, None)]).T
        yb_ref[0, :, hb * HEAD_DIM:(hb + 1) * HEAD_DIM] = o.astype(_BF16)
    for kv in range(KV_C):
        sink = sink_ref[0, kv]
        sink2 = jnp.concatenate(
            [jnp.broadcast_to(sink[g:g + 1, :1], (1, n)) for g in range(G_C)], axis=1) * LOG2_E
        o = _attend_t(heads(QC0 + kv * G_C, G_C), [(head(KC0 + kv), head(VC0 + kv), None, None)], sink2).T
        for g in range(G_C):
            u = kv * G_C + g
            yc_ref[0, :, u * HEAD_DIM:(u + 1) * HEAD_DIM] = o[g * n:(g + 1) * n].astype(_BF16)


def _attn_ctx(qkv_ctx, sink_tabs, layer, bsz):
    n_heads, rows = qkv_ctx.shape[1:3]
    n = rows // bsz
    widths = (H_A * HEAD_DIM, H_B * HEAD_DIM, H_C * HEAD_DIM)
    return pl.pallas_call(
        _attn_ctx_kernel,
        grid=(bsz,),
        in_specs=[pl.BlockSpec((1, n_heads, n, HEAD_DIM), lambda b: (0, 0, b, 0)),
                  pl.BlockSpec((1,) + sink_tabs.shape[1:], lambda b: (layer, 0, 0, 0))],
        out_specs=[pl.BlockSpec((1, n, w), lambda b: (0, b, 0)) for w in widths],
        out_shape=[jax.ShapeDtypeStruct((1, rows, w), _BF16) for w in widths],
        compiler_params=_cparams("parallel"),
        name="attn_ctx",
    )(qkv_ctx, sink_tabs)


def _outproj_kernel(ya_ref, yb_ref, yc_ref, x_ref, mod_ref, g_ref, w_ref, xo_ref, h_ref):
    tm, d = x_ref.shape[1:]
    mod = mod_ref[0, 0]
    chunks = [slice(m0, min(m0 + OUTPROJ_ROW_CHUNK, tm)) for m0 in range(0, tm, OUTPROJ_ROW_CHUNK)]

    def project(rows):
        acc = None
        r0 = 0
        for y_ref in (ya_ref, yb_ref, yc_ref):
            r1 = r0 + y_ref.shape[-1]
            part = jnp.dot(y_ref[0, rows, :], w_ref[0, r0:r1, :], preferred_element_type=_F32)
            acc = part if acc is None else acc + part
            r0 = r1
        return acc

    acc_next = project(chunks[0])
    for i, rows in enumerate(chunks):
        acc = acc_next
        if i + 1 < len(chunks):
            acc_next = project(chunks[i + 1])
        xn = x_ref[0, rows, :] + mod[:, 2 * d:3 * d] * acc
        xo_ref[0, rows, :] = xn
        h_ref[0, rows, :] = _norm_mod(xn, g_ref[0], mod[:, 3 * d:4 * d],
                                      mod[:, 4 * d:5 * d]).astype(_BF16)


def _outproj(ya, yb, yc, x, mod, layer, mod_row, gain, w_out, tm):
    bsz, length, d = x.shape
    tm = min(tm, length)
    row = lambda b, i: (b, i, 0)
    return pl.pallas_call(
        _outproj_kernel,
        grid=(bsz, length // tm),
        in_specs=[pl.BlockSpec((1, tm, y.shape[-1]), row) for y in (ya, yb, yc)] + [
            pl.BlockSpec((1, tm, d), row),
            _mod_spec(mod, layer, mod_row, 2),
            pl.BlockSpec((1, 1, d), lambda b, i: (layer, 0, 0)),
            pl.BlockSpec((1,) + w_out.shape[1:], lambda b, i: (layer, 0, 0), pipeline_mode=pl.Buffered(1)),
        ],
        out_specs=[pl.BlockSpec((1, tm, d), row), pl.BlockSpec((1, tm, d), row)],
        out_shape=[jax.ShapeDtypeStruct((bsz, length, d), _F32),
                   jax.ShapeDtypeStruct((bsz, length, d), _BF16)],
        compiler_params=_cparams("parallel", "parallel"),
        name="out_proj_residual",
    )(ya, yb, yc, x, mod, gain, w_out)


def _mlp_kernel(h_ref, x_ref, mod_ref, wu_ref, wd_ref, o_ref):
    k = pl.program_id(2)
    d = x_ref.shape[-1]
    @pl.when(k == 0)
    def _():
        o_ref[...] = jnp.zeros_like(o_ref)

    u = jnp.dot(h_ref[0], wu_ref[0], preferred_element_type=_F32)
    act = jnp.square(jnp.maximum(u, 0.0)).astype(_BF16)
    o_ref[0] += jnp.dot(act, wd_ref[0], preferred_element_type=_F32)

    @pl.when(k == pl.num_programs(2) - 1)
    def _():
        o_ref[0] = x_ref[0] + mod_ref[0, 0, :, 5 * d:6 * d] * o_ref[0]


def _mlp(h, x, mod, layer, mod_row, w_up, w_down, tm, tf):
    bsz, length, d = x.shape
    d_ff = w_up.shape[-1]
    tm = min(tm, length)
    tf = min(tf, d_ff)
    row = lambda b, i, k: (b, i, 0)
    return pl.pallas_call(
        _mlp_kernel,
        grid=(bsz, length // tm, d_ff // tf),
        in_specs=[pl.BlockSpec((1, tm, d), row),
                  pl.BlockSpec((1, tm, d), row),
                  _mod_spec(mod, layer, mod_row, 3),
                  pl.BlockSpec((1, d, tf), lambda b, i, k: (layer, 0, k)),
                  pl.BlockSpec((1, tf, d), lambda b, i, k: (layer, k, 0))],
        out_specs=pl.BlockSpec((1, tm, d), row),
        out_shape=jax.ShapeDtypeStruct((bsz, length, d), _F32),
        compiler_params=_cparams("parallel", "parallel", "arbitrary"),
        name="mlp_sq_relu",
    )(h, x, mod, w_up, w_down)


def _rope_tables(length):
    t = np.arange(length)
    nf = HEAD_DIM // 4
    inv = ROPE_THETA ** (-jnp.arange(nf, dtype=_F32) / nf)
    ar = jnp.asarray(t // GRID_W, _F32)[:, None] * inv
    ac = jnp.asarray(t % GRID_W, _F32)[:, None] * inv
    ang = jnp.concatenate([ar, ar, ac, ac], axis=-1)
    cos, sin = jnp.cos(ang), jnp.sin(ang)
    first = (np.arange(HEAD_DIM) // nf) % 2 == 0
    return cos, jnp.where(first, -sin, 0.0), jnp.where(first, 0.0, sin)


def kernel(x, c, ctx, c_ctx, w_mod, b_mod, norm_attn, norm_mlp, w_in, q_norm, k_norm,
           rel_pos_bias, sink_logits, w_out, w_up, w_down):
    bsz, length, d = x.shape
    depth = w_mod.shape[0]
    assert length % (NB_QROWS * GRID_W) == 0 and length // GRID_W >= NB_KROWS
    assert w_in.shape[-1] == N_HEADS_IN * HEAD_DIM and w_out.shape[1] == N_HEADS_OUT * HEAD_DIM

    mod_rows = 8
    c_rows = jnp.zeros((mod_rows, d), _F32).at[:bsz].set(c).at[bsz].set(c_ctx)
    mod = _modulation(c_rows, w_mod, b_mod).reshape(depth, mod_rows, 1, N_MOD * d)

    w_in_b, w_out_b = w_in.astype(_BF16), w_out.astype(_BF16)
    w_up_b, w_down_b = w_up.astype(_BF16), w_down.astype(_BF16)
    g_attn = norm_attn.reshape(depth, 1, d)
    g_mlp = norm_mlp.reshape(depth, 1, d)
    rope_tabs = _rope_tables(length)
    bias_tabs = _nb_bias_tables(rel_pos_bias, length // GRID_W)
    sink_tabs = jnp.broadcast_to(
        sink_logits.reshape(depth, KV_C, G_C, 1), (depth, KV_C, G_C, HEAD_DIM)).astype(_F32)

    cs = ctx.reshape(1, bsz * ctx.shape[1], d)
    for l in range(depth):
        last = l == depth - 1
        qkv, vt = _qkv_proj(x, mod, l, None, g_attn, w_in_b, q_norm, k_norm, rope_tabs, tm=256)
        qkv_c, vt_c = _qkv_proj(cs, mod, l, bsz, g_attn, w_in_b, q_norm, k_norm, None, tm=512)
        ya = _attn_a(qkv, vt, qkv_c, vt_c, tq=512, tk=512)
        yb = _attn_b(qkv, qkv_c, bias_tabs, l, tiles=4)
        yc = _attn_c(qkv, qkv_c, sink_tabs, l, tq=512)
        x, h = _outproj(ya, yb, yc, x, mod, l, None, g_mlp, w_out_b, tm=512)
        x = _mlp(h, x, mod, l, None, w_up_b, w_down_b, tm=512, tf=1024)
        if not last:
            ya_c, yb_c, yc_c = _attn_ctx(qkv_c, sink_tabs, l, bsz)
            cs, hc = _outproj(ya_c, yb_c, yc_c, cs, mod, l, bsz, g_mlp, w_out_b, tm=512)
            cs = _mlp(hc, cs, mod, l, bsz, w_up_b, w_down_b, tm=512, tf=1024)
    return x
```

```python
import functools

import numpy as np
import jax
import jax.numpy as jnp
from jax import lax
from jax.experimental import pallas as pl
from jax.experimental.pallas import tpu as pltpu

HEAD_DIM = 128
H_A, KV_A = 4, 2
H_B = 6
H_C, KV_C = 6, 2
G_A = H_A // KV_A
G_C = H_C // KV_C
GRID_W = 64
NA_ROWS = 8
NA_COLS = 16
WINDOW = 128
ROPE_THETA = 10000.0
EPS = 1e-6
N_MOD = 6
ATTN_SCALE = HEAD_DIM ** -0.5
NEG_INF = -1e30
LOG2_E = 1.4426950408889634
LOGIT2_SCALE = ATTN_SCALE * LOG2_E

_IN_QA, _IN_KA, _IN_VA = 0, H_A, H_A + KV_A
_IN_QB = _IN_VA + KV_A
_IN_KB, _IN_VB = _IN_QB + H_B, _IN_QB + 2 * H_B
_IN_QC = _IN_VB + H_B
_IN_KC, _IN_VC = _IN_QC + H_C, _IN_QC + H_C + KV_C
N_HEADS_IN = _IN_VC + KV_C
N_HEADS_OUT = H_A + H_B + H_C

QC0 = 0
QA0 = QC0 + H_C
KA0 = QA0 + H_A
VA0 = KA0 + KV_A
QB0 = VA0 + KV_A
KB0 = QB0 + H_B
VB0 = KB0 + H_B
KC0 = VB0 + H_B
VC0 = KC0 + KV_C
assert QA0 % G_A == 0 and QC0 % G_C == 0 and VC0 + KV_C == N_HEADS_IN

_REGIONS = (
    (_IN_QA, QA0, H_A, "q", 0, True), (_IN_KA, KA0, KV_A, "k", 0, True), (_IN_VA, VA0, KV_A, None, 0, False),
    (_IN_QB, QB0, H_B, "q", 1, False), (_IN_KB, KB0, H_B, "k", 1, False), (_IN_VB, VB0, H_B, None, 0, False),
    (_IN_QC, QC0, H_C, "q", 2, True), (_IN_KC, KC0, KV_C, "k", 2, True), (_IN_VC, VC0, KV_C, None, 0, False),
)

VT_ROWS = HEAD_DIM + 16
NB_QROWS = 4
NB_KROWS = NB_QROWS + NA_ROWS - 1
NB_HEADS_PER_STEP = 2
OUTPROJ_ROW_CHUNK = 256
VMEM_LIMIT_BYTES = 56 * 1024 * 1024

_BF16 = jnp.bfloat16
_F32 = jnp.float32
_NT = (((1,), (1,)), ((), ()))
_TN = (((0,), (0,)), ((), ()))


def _cparams(*sem):
    return pltpu.CompilerParams(dimension_semantics=sem, vmem_limit_bytes=VMEM_LIMIT_BYTES)


def _norm_mod(x, gain, shift, scale):
    y = x * lax.rsqrt(jnp.mean(x * x, axis=-1, keepdims=True) + EPS)
    return (y * gain) * (1.0 + scale) + shift


def _logits2_t(k, q):
    return lax.dot_general(k, q, _NT, preferred_element_type=_F32)


def _attend_logits(q, kvs):
    zs = []
    for k, _, bias2, keep in kvs:
        z = _logits2_t(k, q)
        if bias2 is not None:
            z = z + bias2
        if keep is not None:
            z = jnp.where(keep, z, NEG_INF)
        zs.append(z)
    return zs


def _attend_t(q, kvs, sink2_row=None, zs=None):
    if zs is None:
        zs = _attend_logits(q, kvs)
    m = zs[0].max(axis=0, keepdims=True)
    for z in zs[1:]:
        m = jnp.maximum(m, z.max(axis=0, keepdims=True))
    if sink2_row is not None:
        m = jnp.maximum(m, sink2_row)
    l = None
    acc = None
    for z, (_, v, _, _) in zip(zs, kvs):
        p = jnp.exp2(z - m)
        ls = p.sum(axis=0, keepdims=True)
        pv = lax.dot_general(v, p.astype(_BF16), _TN, preferred_element_type=_F32)
        l = ls if l is None else l + ls
        acc = pv if acc is None else acc + pv
    if sink2_row is not None:
        l = l + jnp.exp2(sink2_row - m)
    return acc / l


def _attend_units(units):
    outs = []
    zs_next = _attend_logits(*units[0][:2])
    for u, (q, kvs, sink2) in enumerate(units):
        zs = zs_next
        if u + 1 < len(units):
            zs_next = _attend_logits(*units[u + 1][:2])
        outs.append(_attend_t(q, kvs, sink2, zs))
    return outs


def _mod_kernel(c_ref, w_ref, b_ref, o_ref):
    c = c_ref[...]
    cond = c * jax.nn.sigmoid(c)
    o_ref[0] = jnp.dot(cond.astype(_BF16), w_ref[0].astype(_BF16),
                       preferred_element_type=_F32) + b_ref[0]


def _modulation(c_rows, w_mod, b_mod):
    depth, d, n = w_mod.shape
    tn = next(t for t in (1024, 512, 256, 128, n) if n % t == 0)
    rows = c_rows.shape[0]
    return pl.pallas_call(
        _mod_kernel,
        grid=(depth, n // tn),
        in_specs=[pl.BlockSpec((rows, d), lambda l, j: (0, 0)),
                  pl.BlockSpec((1, d, tn), lambda l, j: (l, 0, j)),
                  pl.BlockSpec((1, 1, tn), lambda l, j: (l, 0, j))],
        out_specs=pl.BlockSpec((1, rows, tn), lambda l, j: (l, 0, j)),
        out_shape=jax.ShapeDtypeStruct((depth, rows, n), _F32),
        compiler_params=_cparams("parallel", "parallel"),
        name="adaln_modulation",
    )(c_rows, w_mod, b_mod.reshape(depth, 1, n))


def _mod_spec(mod, layer, mod_row, n_grid):
    block = (1, 1, 1, mod.shape[-1])
    if n_grid == 2:
        return pl.BlockSpec(block, (lambda b, i: (layer, b, 0, 0)) if mod_row is None
                            else (lambda b, i: (layer, mod_row, 0, 0)))
    return pl.BlockSpec(block, (lambda b, i, k: (layer, b, 0, 0)) if mod_row is None
                        else (lambda b, i, k: (layer, mod_row, 0, 0)))


def _qkv_kernel(x_ref, mod_ref, g_ref, w_ref, qn_ref, kn_ref, *rest, rope):
    if rope:
        cos_ref, sa_ref, sb_ref, o_ref, vt_ref = rest
    else:
        o_ref, vt_ref = rest
    d = x_ref.shape[-1]
    mod = mod_ref[0, 0]
    h = _norm_mod(x_ref[0], g_ref[0], mod[:, 0:d], mod[:, d:2 * d]).astype(_BF16)
    for c0, o0, nh, kind, row, rotary in _REGIONS:
        y = jnp.dot(h, w_ref[0, :, c0 * HEAD_DIM:(c0 + nh) * HEAD_DIM], preferred_element_type=_F32)
        for j in range(nh):
            yj = y[:, j * HEAD_DIM:(j + 1) * HEAD_DIM]
            if kind is not None:
                gain = (qn_ref if kind == "q" else kn_ref)[0, row:row + 1, :]
                yj = yj * lax.rsqrt(jnp.mean(yj * yj, axis=-1, keepdims=True) + EPS) * gain
                if rope and rotary:
                    yj = (yj * cos_ref[...]
                          + pltpu.roll(yj, HEAD_DIM - HEAD_DIM // 4, 1) * sa_ref[...]
                          + pltpu.roll(yj, HEAD_DIM // 4, 1) * sb_ref[...])
                if kind == "q":
                    yj = yj * LOGIT2_SCALE
            o_ref[0, o0 + j] = yj.astype(_BF16)
            if o0 == VA0:
                vt_ref[0, j, :HEAD_DIM, :] = yj.T.astype(_BF16)
                vt_ref[0, j, HEAD_DIM:, :] = jnp.ones((VT_ROWS - HEAD_DIM, yj.shape[0]), _BF16)


def _qkv_proj(x, mod, layer, mod_row, gain, w_in, qn, kn, rope_tabs, tm):
    bsz, length, d = x.shape
    d_in = w_in.shape[-1]
    tm = min(tm, length)
    in_specs = [
        pl.BlockSpec((1, tm, d), lambda b, i: (b, i, 0)),
        _mod_spec(mod, layer, mod_row, 2),
        pl.BlockSpec((1, 1, d), lambda b, i: (layer, 0, 0)),
        pl.BlockSpec((1, d, d_in), lambda b, i: (layer, 0, 0), pipeline_mode=pl.Buffered(1)),
        pl.BlockSpec((1,) + qn.shape[1:], lambda b, i: (layer, 0, 0)),
        pl.BlockSpec((1,) + kn.shape[1:], lambda b, i: (layer, 0, 0)),
    ]
    args = [x, mod, gain, w_in, qn, kn]
    if rope_tabs is not None:
        in_specs += [pl.BlockSpec((tm, HEAD_DIM), lambda b, i: (i, 0))] * 3
        args += list(rope_tabs)
    return pl.pallas_call(
        functools.partial(_qkv_kernel, rope=rope_tabs is not None),
        grid=(bsz, length // tm),
        in_specs=in_specs,
        out_specs=[pl.BlockSpec((1, N_HEADS_IN, tm, HEAD_DIM), lambda b, i: (b, 0, i, 0)),
                   pl.BlockSpec((1, KV_A, VT_ROWS, tm), lambda b, i: (b, 0, 0, i))],
        out_shape=[jax.ShapeDtypeStruct((bsz, N_HEADS_IN, length, HEAD_DIM), _BF16),
                   jax.ShapeDtypeStruct((bsz, KV_A, VT_ROWS, length), _BF16)],
        compiler_params=_cparams("parallel", "parallel"),
        name="qkv_proj_rope" if rope_tabs is not None else "qkv_proj_ctx",
    )(*args)


def _head_spec(rows, head0, heads=1):
    return pl.BlockSpec((1, heads, rows, HEAD_DIM), lambda b, h, t: (b, head0 // heads + h, 0, 0))


def _ctx_head_spec(rows, head0, heads=1):
    return pl.BlockSpec((1, heads, rows, HEAD_DIM), lambda b, h, t: (0, head0 // heads + h, b, 0))


def _query_spec(rows, head0, heads):
    return pl.BlockSpec((1, heads, rows, HEAD_DIM), lambda b, h, t: (b, head0 // heads + h, t, 0))


def _attn_a_kernel(q_ref, k_ref, vt_ref, kc_ref, vtc_ref, o_ref, acc_sc, s0_sc, s1_sc, *, tk):
    tq = q_ref.shape[2]
    length = k_ref.shape[2]
    nq = G_A * tq
    q2 = q_ref[0].reshape(nq, HEAD_DIM)

    def qk(kb):
        return _logits2_t(kb, q2)

    def softmax_pv(s, vb, m_prev):
        m_new = jnp.maximum(m_prev, s.max(axis=0, keepdims=True))
        alpha = jnp.exp2(m_prev - m_new)
        p = jnp.exp2(s - m_new)
        pv = jnp.dot(vb, p.astype(_BF16), preferred_element_type=_F32)
        acc_sc[...] = alpha * acc_sc[...] + pv
        return m_new

    def keys(ref, j):
        return ref[0, 0, pl.ds(pl.multiple_of(j * tk, tk), tk), :]

    def values_t(j):
        return vt_ref[0, 0, :, pl.ds(pl.multiple_of(j * tk, tk), tk)]

    n_blocks = length // tk
    acc_sc[...] = jnp.zeros_like(acc_sc)
    m0 = jnp.full((1, nq), NEG_INF, _F32)
    s0_sc[...] = qk(keys(k_ref, 0))
    m = softmax_pv(qk(kc_ref[0, 0]), vtc_ref[0, 0], m0)

    def pair(j0, m, last):
        s1_sc[...] = qk(keys(k_ref, j0 + 1))
        m = softmax_pv(s0_sc[...], values_t(j0), m)
        if not last:
            s0_sc[...] = qk(keys(k_ref, j0 + 2))
        return softmax_pv(s1_sc[...], values_t(j0 + 1), m)

    m = lax.fori_loop(0, n_blocks // 2 - 1, lambda jj, m: pair(2 * jj, m, False), m)
    pair(n_blocks - 2, m, True)
    o = (acc_sc[:HEAD_DIM, :] / acc_sc[HEAD_DIM:HEAD_DIM + 1, :]).T
    o_ref[0] = jnp.concatenate([o[g * tq:(g + 1) * tq] for g in range(G_A)], axis=1).astype(_BF16)


def _attn_a(qkv, vt, qkv_ctx, vt_ctx, tq, tk):
    bsz, _, length, _ = qkv.shape
    n_ctx = qkv_ctx.shape[2] // bsz
    tq = min(tq, length)
    tk = min(tk, length // 2)
    assert length % (2 * tk) == 0
    return pl.pallas_call(
        functools.partial(_attn_a_kernel, tk=tk),
        grid=(bsz, KV_A, length // tq),
        in_specs=[_query_spec(tq, QA0, G_A), _head_spec(length, KA0),
                  pl.BlockSpec((1, 1, VT_ROWS, length), lambda b, h, i: (b, h, 0, 0)),
                  _ctx_head_spec(n_ctx, KA0),
                  pl.BlockSpec((1, 1, VT_ROWS, n_ctx), lambda b, h, i: (0, h, 0, b))],
        out_specs=pl.BlockSpec((1, tq, G_A * HEAD_DIM), lambda b, h, i: (b, i, h)),
        out_shape=jax.ShapeDtypeStruct((bsz, length, H_A * HEAD_DIM), _BF16),
        scratch_shapes=[pltpu.VMEM((VT_ROWS, G_A * tq), _F32),
                        pltpu.VMEM((tk, G_A * tq), _F32), pltpu.VMEM((tk, G_A * tq), _F32)],
        compiler_params=_cparams("parallel", "parallel", "parallel"),
        name="attn_global",
    )(qkv, qkv, vt, qkv_ctx, vt_ctx)


def _attn_b_kernel(q_ref, k_ref, v_ref, kc_ref, vc_ref, bias_ref, o_ref):
    n_rows = k_ref.shape[2] // GRID_W
    tq = NB_QROWS * GRID_W
    nk = NB_KROWS * GRID_W
    tiles = q_ref.shape[2] // tq
    n_tiles = pl.num_programs(2) * tiles
    units = []
    for i in range(tiles):
        t = pl.program_id(2) * tiles + i
        ks = jnp.clip(t * NB_QROWS - NA_ROWS // 2, 0, n_rows - NB_KROWS)
        off = pl.multiple_of(ks * GRID_W, GRID_W)
        variant = jnp.where(t == 0, 0, jnp.where(t == n_tiles - 1, 2, 1))
        for hh in range(NB_HEADS_PER_STEP):
            kvs = [(k_ref[0, hh, pl.ds(off, nk), :], v_ref[0, hh, pl.ds(off, nk), :],
                    bias_ref[0, hh, variant], None),
                   (kc_ref[0, hh], vc_ref[0, hh], None, None)]
            units.append((q_ref[0, hh, i * tq:(i + 1) * tq, :], kvs, None))
    outs = _attend_units(units)
    for i in range(tiles):
        heads = outs[i * NB_HEADS_PER_STEP:(i + 1) * NB_HEADS_PER_STEP]
        o_ref[0, i * tq:(i + 1) * tq, :] = jnp.concatenate([o.T for o in heads], axis=1).astype(_BF16)


def _nb_bias_tables(rpb, n_rows):
    kh = NA_ROWS
    c = np.arange(GRID_W)[:, None, None]
    j = np.arange(2 * NA_COLS - 1)[None, :, None]
    ck = np.arange(GRID_W)[None, None, :]
    col_onehot = (j == ck - c + (NA_COLS - 1)).astype(np.float32)
    cs = np.clip(c - NA_COLS // 2, 0, GRID_W - NA_COLS)
    col_ok = ((ck >= cs) & (ck < cs + NA_COLS))[:, 0, :]
    blocks = jnp.einsum("lhrj,cjk->lhrkc", rpb * LOG2_E, col_onehot, precision=lax.Precision.HIGHEST)
    blocks = jnp.where(col_ok.T, blocks, NEG_INF).astype(_F32)
    masked = jnp.full(blocks.shape[:2] + (GRID_W, GRID_W), NEG_INF, _F32)
    tabs = []
    for r0, ks in ((0, 0), (NB_QROWS, 0), (n_rows - NB_QROWS, n_rows - NB_KROWS)):
        key_rows = []
        for i in range(NB_KROWS):
            row = []
            for a in range(NB_QROWS):
                rq, rk = r0 + a, ks + i
                rs = min(max(rq - kh // 2, 0), n_rows - kh)
                row.append(blocks[:, :, rk - rq + (NA_ROWS - 1)] if rs <= rk < rs + kh else masked)
            key_rows.append(jnp.concatenate(row, axis=-1))
        tabs.append(jnp.concatenate(key_rows, axis=-2))
    return jnp.stack(tabs, axis=2)


def _attn_b(qkv, qkv_ctx, bias_tabs, layer, tiles):
    bsz, _, length, _ = qkv.shape
    n_ctx = qkv_ctx.shape[2] // bsz
    tq = min(tiles * NB_QROWS * GRID_W, length)
    assert length % tq == 0
    hps = NB_HEADS_PER_STEP
    return pl.pallas_call(
        _attn_b_kernel,
        grid=(bsz, H_B // hps, length // tq),
        in_specs=[_query_spec(tq, QB0, hps), _head_spec(length, KB0, hps), _head_spec(length, VB0, hps),
                  _ctx_head_spec(n_ctx, KB0, hps), _ctx_head_spec(n_ctx, VB0, hps),
                  pl.BlockSpec((1, hps) + bias_tabs.shape[2:], lambda b, h, t: (layer, h, 0, 0, 0))],
        out_specs=pl.BlockSpec((1, tq, hps * HEAD_DIM), lambda b, h, t: (b, t, h)),
        out_shape=jax.ShapeDtypeStruct((bsz, length, H_B * HEAD_DIM), _BF16),
        compiler_params=_cparams("parallel", "parallel", "parallel"),
        name="attn_neighbourhood",
    )(qkv, qkv, qkv, qkv_ctx, qkv_ctx, bias_tabs)


def _sink2_row(sink):
    return jnp.concatenate([sink[g:g + 1, :] for g in range(G_C)], axis=1) * LOG2_E


def _attn_c_kernel(q_ref, k_ref, v_ref, kc_ref, vc_ref, sink_ref, o_ref):
    t = pl.program_id(2)
    tq = q_ref.shape[2]
    length = k_ref.shape[2]
    band = 3 * WINDOW
    nq = G_C * WINDOW
    kc = kc_ref[0, 0]
    vc = vc_ref[0, 0]
    sink2 = _sink2_row(sink_ref[0, 0])
    rel = (lax.broadcasted_iota(jnp.int32, (band, nq), 1) % WINDOW
           - lax.broadcasted_iota(jnp.int32, (band, nq), 0))
    units = []
    n_sub = tq // WINDOW
    keep_centred = jnp.abs(rel + WINDOW) <= WINDOW
    for n in range(n_sub):
        i0 = t * tq + n * WINDOW
        start = pl.multiple_of(jnp.clip(i0 - WINDOW, 0, length - band), WINDOW)
        kb = k_ref[0, 0, pl.ds(start, band), :]
        vb = v_ref[0, 0, pl.ds(start, band), :]
        q3 = q_ref[0, :, n * WINDOW:(n + 1) * WINDOW, :].reshape(nq, HEAD_DIM)
        keep = keep_centred if 0 < n < n_sub - 1 else jnp.abs(rel + (i0 - start)) <= WINDOW
        units.append((q3, [(kb, vb, None, keep), (kc, vc, None, None)], sink2))
    for n, o_t in enumerate(_attend_units(units)):
        o = o_t.T
        o_ref[0, n * WINDOW:(n + 1) * WINDOW, :] = jnp.concatenate(
            [o[g * WINDOW:(g + 1) * WINDOW] for g in range(G_C)], axis=1).astype(_BF16)


def _attn_c(qkv, qkv_ctx, sink_tabs, layer, tq):
    bsz, _, length, _ = qkv.shape
    n_ctx = qkv_ctx.shape[2] // bsz
    tq = min(tq, length)
    return pl.pallas_call(
        _attn_c_kernel,
        grid=(bsz, KV_C, length // tq),
        in_specs=[_query_spec(tq, QC0, G_C), _head_spec(length, KC0), _head_spec(length, VC0),
                  _ctx_head_spec(n_ctx, KC0), _ctx_head_spec(n_ctx, VC0),
                  pl.BlockSpec((1, 1, G_C, HEAD_DIM), lambda b, h, t: (layer, h, 0, 0))],
        out_specs=pl.BlockSpec((1, tq, G_C * HEAD_DIM), lambda b, h, t: (b, t, h)),
        out_shape=jax.ShapeDtypeStruct((bsz, length, H_C * HEAD_DIM), _BF16),
        compiler_params=_cparams("parallel", "parallel", "parallel"),
        name="attn_window",
    )(qkv, qkv, qkv, qkv_ctx, qkv_ctx, sink_tabs)


def _attn_ctx_kernel(qkv_ref, sink_ref, ya_ref, yb_ref, yc_ref):
    n = qkv_ref.shape[2]

    def head(u):
        return qkv_ref[0, u]

    def heads(u0, g):
        return qkv_ref[0, u0:u0 + g].reshape(g * n, HEAD_DIM)

    for kv in range(KV_A):
        o = _attend_t(heads(QA0 + kv * G_A, G_A), [(head(KA0 + kv), head(VA0 + kv), None, None)]).T
        for g in range(G_A):
            u = kv * G_A + g
            ya_ref[0, :, u * HEAD_DIM:(u + 1) * HEAD_DIM] = o[g * n:(g + 1) * n].astype(_BF16)
    for hb in range(H_B):
        o = _attend_t(head(QB0 + hb), [(head(KB0 + hb), head(VB0 + hb), None, None)]).T
        yb_ref[0, :, hb * HEAD_DIM:(hb + 1) * HEAD_DIM] = o.astype(_BF16)
    for kv in range(KV_C):
        sink = sink_ref[0, kv]
        sink2 = jnp.concatenate(
            [jnp.broadcast_to(sink[g:g + 1, :1], (1, n)) for g in range(G_C)], axis=1) * LOG2_E
        o = _attend_t(heads(QC0 + kv * G_C, G_C), [(head(KC0 + kv), head(VC0 + kv), None, None)], sink2).T
        for g in range(G_C):
            u = kv * G_C + g
            yc_ref[0, :, u * HEAD_DIM:(u + 1) * HEAD_DIM] = o[g * n:(g + 1) * n].astype(_BF16)


def _attn_ctx(qkv_ctx, sink_tabs, layer, bsz):
    n_heads, rows = qkv_ctx.shape[1:3]
    n = rows // bsz
    widths = (H_A * HEAD_DIM, H_B * HEAD_DIM, H_C * HEAD_DIM)
    return pl.pallas_call(
        _attn_ctx_kernel,
        grid=(bsz,),
        in_specs=[pl.BlockSpec((1, n_heads, n, HEAD_DIM), lambda b: (0, 0, b, 0)),
                  pl.BlockSpec((1,) + sink_tabs.shape[1:], lambda b: (layer, 0, 0, 0))],
        out_specs=[pl.BlockSpec((1, n, w), lambda b: (0, b, 0)) for w in widths],
        out_shape=[jax.ShapeDtypeStruct((1, rows, w), _BF16) for w in widths],
        compiler_params=_cparams("parallel"),
        name="attn_ctx",
    )(qkv_ctx, sink_tabs)


def _outproj_kernel(ya_ref, yb_ref, yc_ref, x_ref, mod_ref, g_ref, w_ref, xo_ref, h_ref):
    tm, d = x_ref.shape[1:]
    mod = mod_ref[0, 0]
    chunks = [slice(m0, min(m0 + OUTPROJ_ROW_CHUNK, tm)) for m0 in range(0, tm, OUTPROJ_ROW_CHUNK)]

    def project(rows):
        acc = None
        r0 = 0
        for y_ref in (ya_ref, yb_ref, yc_ref):
            r1 = r0 + y_ref.shape[-1]
            part = jnp.dot(y_ref[0, rows, :], w_ref[0, r0:r1, :], preferred_element_type=_F32)
            acc = part if acc is None else acc + part
            r0 = r1
        return acc

    acc_next = project(chunks[0])
    for i, rows in enumerate(chunks):
        acc = acc_next
        if i + 1 < len(chunks):
            acc_next = project(chunks[i + 1])
        xn = x_ref[0, rows, :] + mod[:, 2 * d:3 * d] * acc
        xo_ref[0, rows, :] = xn
        h_ref[0, rows, :] = _norm_mod(xn, g_ref[0], mod[:, 3 * d:4 * d],
                                      mod[:, 4 * d:5 * d]).astype(_BF16)


def _outproj(ya, yb, yc, x, mod, layer, mod_row, gain, w_out, tm):
    bsz, length, d = x.shape
    tm = min(tm, length)
    row = lambda b, i: (b, i, 0)
    return pl.pallas_call(
        _outproj_kernel,
        grid=(bsz, length // tm),
        in_specs=[pl.BlockSpec((1, tm, y.shape[-1]), row) for y in (ya, yb, yc)] + [
            pl.BlockSpec((1, tm, d), row),
            _mod_spec(mod, layer, mod_row, 2),
            pl.BlockSpec((1, 1, d), lambda b, i: (layer, 0, 0)),
            pl.BlockSpec((1,) + w_out.shape[1:], lambda b, i: (layer, 0, 0), pipeline_mode=pl.Buffered(1)),
        ],
        out_specs=[pl.BlockSpec((1, tm, d), row), pl.BlockSpec((1, tm, d), row)],
        out_shape=[jax.ShapeDtypeStruct((bsz, length, d), _F32),
                   jax.ShapeDtypeStruct((bsz, length, d), _BF16)],
        compiler_params=_cparams("parallel", "parallel"),
        name="out_proj_residual",
    )(ya, yb, yc, x, mod, gain, w_out)


def _mlp_kernel(h_ref, x_ref, mod_ref, wu_ref, wd_ref, o_ref):
    k = pl.program_id(2)
    d = x_ref.shape[-1]
    @pl.when(k == 0)
    def _():
        o_ref[...] = jnp.zeros_like(o_ref)

    u = jnp.dot(h_ref[0], wu_ref[0], preferred_element_type=_F32)
    act = jnp.square(jnp.maximum(u, 0.0)).astype(_BF16)
    o_ref[0] += jnp.dot(act, wd_ref[0], preferred_element_type=_F32)

    @pl.when(k == pl.num_programs(2) - 1)
    def _():
        o_ref[0] = x_ref[0] + mod_ref[0, 0, :, 5 * d:6 * d] * o_ref[0]


def _mlp(h, x, mod, layer, mod_row, w_up, w_down, tm, tf):
    bsz, length, d = x.shape
    d_ff = w_up.shape[-1]
    tm = min(tm, length)
    tf = min(tf, d_ff)
    row = lambda b, i, k: (b, i, 0)
    return pl.pallas_call(
        _mlp_kernel,
        grid=(bsz, length // tm, d_ff // tf),
        in_specs=[pl.BlockSpec((1, tm, d), row),
                  pl.BlockSpec((1, tm, d), row),
                  _mod_spec(mod, layer, mod_row, 3),
                  pl.BlockSpec((1, d, tf), lambda b, i, k: (layer, 0, k)),
                  pl.BlockSpec((1, tf, d), lambda b, i, k: (layer, k, 0))],
        out_specs=pl.BlockSpec((1, tm, d), row),
        out_shape=jax.ShapeDtypeStruct((bsz, length, d), _F32),
        compiler_params=_cparams("parallel", "parallel", "arbitrary"),
        name="mlp_sq_relu",
    )(h, x, mod, w_up, w_down)


def _rope_tables(length):
    t = np.arange(length)
    nf = HEAD_DIM // 4
    inv = ROPE_THETA ** (-jnp.arange(nf, dtype=_F32) / nf)
    ar = jnp.asarray(t // GRID_W, _F32)[:, None] * inv
    ac = jnp.asarray(t % GRID_W, _F32)[:, None] * inv
    ang = jnp.concatenate([ar, ar, ac, ac], axis=-1)
    cos, sin = jnp.cos(ang), jnp.sin(ang)
    first = (np.arange(HEAD_DIM) // nf) % 2 == 0
    return cos, jnp.where(first, -sin, 0.0), jnp.where(first, 0.0, sin)


def kernel(x, c, ctx, c_ctx, w_mod, b_mod, norm_attn, norm_mlp, w_in, q_norm, k_norm,
           rel_pos_bias, sink_logits, w_out, w_up, w_down):
    bsz, length, d = x.shape
    depth = w_mod.shape[0]
    assert length % (NB_QROWS * GRID_W) == 0 and length // GRID_W >= NB_KROWS
    assert w_in.shape[-1] == N_HEADS_IN * HEAD_DIM and w_out.shape[1] == N_HEADS_OUT * HEAD_DIM

    mod_rows = 8
    c_rows = jnp.zeros((mod_rows, d), _F32).at[:bsz].set(c).at[bsz].set(c_ctx)
    mod = _modulation(c_rows, w_mod, b_mod).reshape(depth, mod_rows, 1, N_MOD * d)

    w_in_b, w_out_b = w_in.astype(_BF16), w_out.astype(_BF16)
    w_up_b, w_down_b = w_up.astype(_BF16), w_down.astype(_BF16)
    g_attn = norm_attn.reshape(depth, 1, d)
    g_mlp = norm_mlp.reshape(depth, 1, d)
    rope_tabs = _rope_tables(length)
    bias_tabs = _nb_bias_tables(rel_pos_bias, length // GRID_W)
    sink_tabs = jnp.broadcast_to(
        sink_logits.reshape(depth, KV_C, G_C, 1), (depth, KV_C, G_C, HEAD_DIM)).astype(_F32)

    cs = ctx.reshape(1, bsz * ctx.shape[1], d)
    for l in range(depth):
        last = l == depth - 1
        qkv, vt = _qkv_proj(x, mod, l, None, g_attn, w_in_b, q_norm, k_norm, rope_tabs, tm=256)
        qkv_c, vt_c = _qkv_proj(cs, mod, l, bsz, g_attn, w_in_b, q_norm, k_norm, None, tm=512)
        ya = _attn_a(qkv, vt, qkv_c, vt_c, tq=512, tk=512)
        yb = _attn_b(qkv, qkv_c, bias_tabs, l, tiles=8)
        yc = _attn_c(qkv, qkv_c, sink_tabs, l, tq=2048)
        x, h = _outproj(ya, yb, yc, x, mod, l, None, g_mlp, w_out_b, tm=512)
        x = _mlp(h, x, mod, l, None, w_up_b, w_down_b, tm=512, tf=1024)
        if not last:
            ya_c, yb_c, yc_c = _attn_ctx(qkv_c, sink_tabs, l, bsz)
            cs, hc = _outproj(ya_c, yb_c, yc_c, cs, mod, l, bsz, g_mlp, w_out_b, tm=512)
            cs = _mlp(hc, cs, mod, l, bsz, w_up_b, w_down_b, tm=512, tf=1024)
    return x
```

```python
import functools

import numpy as np
import jax
import jax.numpy as jnp
from jax import lax
from jax.experimental import pallas as pl
from jax.experimental.pallas import tpu as pltpu

HEAD_DIM = 128
H_A, KV_A = 4, 2
H_B = 6
H_C, KV_C = 6, 2
G_A = H_A // KV_A
G_C = H_C // KV_C
GRID_W = 64
NA_ROWS = 8
NA_COLS = 16
WINDOW = 128
ROPE_THETA = 10000.0
EPS = 1e-6
N_MOD = 6
ATTN_SCALE = HEAD_DIM ** -0.5
NEG_INF = -1e30
LOG2_E = 1.4426950408889634
LOGIT2_SCALE = ATTN_SCALE * LOG2_E
BOUND_MARGIN = 1.02
MIN_ROW_SUM = 2.0 ** -64

_IN_QA, _IN_KA, _IN_VA = 0, H_A, H_A + KV_A
_IN_QB = _IN_VA + KV_A
_IN_KB, _IN_VB = _IN_QB + H_B, _IN_QB + 2 * H_B
_IN_QC = _IN_VB + H_B
_IN_KC, _IN_VC = _IN_QC + H_C, _IN_QC + H_C + KV_C
N_HEADS_IN = _IN_VC + KV_C
N_HEADS_OUT = H_A + H_B + H_C

QC0 = 0
QA0 = QC0 + H_C
KA0 = QA0 + H_A
VA0 = KA0 + KV_A
QB0 = VA0 + KV_A
KB0 = QB0 + H_B
VB0 = KB0 + H_B
KC0 = VB0 + H_B
VC0 = KC0 + KV_C
assert QA0 % G_A == 0 and QC0 % G_C == 0 and VC0 + KV_C == N_HEADS_IN

_REGIONS = (
    (_IN_QA, QA0, H_A, "q", 0, True), (_IN_KA, KA0, KV_A, "k", 0, True), (_IN_VA, VA0, KV_A, None, 0, False),
    (_IN_QB, QB0, H_B, "q", 1, False), (_IN_KB, KB0, H_B, "k", 1, False), (_IN_VB, VB0, H_B, None, 0, False),
    (_IN_QC, QC0, H_C, "q", 2, True), (_IN_KC, KC0, KV_C, "k", 2, True), (_IN_VC, VC0, KV_C, None, 0, False),
)

VT_ROWS = HEAD_DIM + 16
NB_QROWS = 4
NB_KROWS = NB_QROWS + NA_ROWS - 1
NB_HEADS_PER_STEP = 2
OUTPROJ_ROW_CHUNK = 256
VMEM_LIMIT_BYTES = 56 * 1024 * 1024

_BF16 = jnp.bfloat16
_F32 = jnp.float32
_NT = (((1,), (1,)), ((), ()))
_TN = (((0,), (0,)), ((), ()))


def _cparams(*sem):
    return pltpu.CompilerParams(dimension_semantics=sem, vmem_limit_bytes=VMEM_LIMIT_BYTES)


def _norm_mod(x, gain, shift, scale):
    y = x * lax.rsqrt(jnp.mean(x * x, axis=-1, keepdims=True) + EPS)
    return (y * gain) * (1.0 + scale) + shift


def _logits2_t(k, q):
    return lax.dot_general(k, q, _NT, preferred_element_type=_F32)


def _attend_logits(q, kvs):
    zs = []
    for k, _, bias2, keep in kvs:
        z = _logits2_t(k, q)
        if bias2 is not None:
            z = z + bias2
        if keep is not None:
            z = jnp.where(keep, z, NEG_INF)
        zs.append(z)
    return zs


def _attend_t(q, kvs, sink2_row=None, zs=None):
    if zs is None:
        zs = _attend_logits(q, kvs)
    m = zs[0].max(axis=0, keepdims=True)
    for z in zs[1:]:
        m = jnp.maximum(m, z.max(axis=0, keepdims=True))
    if sink2_row is not None:
        m = jnp.maximum(m, sink2_row)
    l = None
    acc = None
    for z, (_, v, _, _) in zip(zs, kvs):
        p = jnp.exp2(z - m)
        ls = p.sum(axis=0, keepdims=True)
        pv = lax.dot_general(v, p.astype(_BF16), _TN, preferred_element_type=_F32)
        l = ls if l is None else l + ls
        acc = pv if acc is None else acc + pv
    if sink2_row is not None:
        l = l + jnp.exp2(sink2_row - m)
    return acc / l


def _attend_units(units):
    outs = []
    zs_next = _attend_logits(*units[0][:2])
    for u, (q, kvs, sink2) in enumerate(units):
        zs = zs_next
        if u + 1 < len(units):
            zs_next = _attend_logits(*units[u + 1][:2])
        outs.append(_attend_t(q, kvs, sink2, zs))
    return outs


def _attend_units_shifted(units, bound):
    outs = []
    min_sum = None
    zs_next = _attend_logits(*units[0][:2])
    for u, (q, kvs, sink2) in enumerate(units):
        zs = zs_next
        if u + 1 < len(units):
            zs_next = _attend_logits(*units[u + 1][:2])
        l = None
        acc = None
        for z, (_, v, _, _) in zip(zs, kvs):
            p = jnp.exp2(z - bound)
            ls = p.sum(axis=0, keepdims=True)
            pv = lax.dot_general(v, p.astype(_BF16), _TN, preferred_element_type=_F32)
            l = ls if l is None else l + ls
            acc = pv if acc is None else acc + pv
        if sink2 is not None:
            l = l + jnp.exp2(sink2 - bound)
        outs.append(acc / l)
        min_sum = l if min_sum is None else jnp.minimum(min_sum, l)
    return outs, jnp.min(min_sum)


def _mod_kernel(c_ref, w_ref, b_ref, o_ref):
    c = c_ref[...]
    cond = c * jax.nn.sigmoid(c)
    o_ref[0] = jnp.dot(cond.astype(_BF16), w_ref[0].astype(_BF16),
                       preferred_element_type=_F32) + b_ref[0]


def _modulation(c_rows, w_mod, b_mod):
    depth, d, n = w_mod.shape
    tn = next(t for t in (1024, 512, 256, 128, n) if n % t == 0)
    rows = c_rows.shape[0]
    return pl.pallas_call(
        _mod_kernel,
        grid=(depth, n // tn),
        in_specs=[pl.BlockSpec((rows, d), lambda l, j: (0, 0)),
                  pl.BlockSpec((1, d, tn), lambda l, j: (l, 0, j)),
                  pl.BlockSpec((1, 1, tn), lambda l, j: (l, 0, j))],
        out_specs=pl.BlockSpec((1, rows, tn), lambda l, j: (l, 0, j)),
        out_shape=jax.ShapeDtypeStruct((depth, rows, n), _F32),
        compiler_params=_cparams("parallel", "parallel"),
        name="adaln_modulation",
    )(c_rows, w_mod, b_mod.reshape(depth, 1, n))


def _mod_spec(mod, layer, mod_row, n_grid):
    block = (1, 1, 1, mod.shape[-1])
    if n_grid == 2:
        return pl.BlockSpec(block, (lambda b, i: (layer, b, 0, 0)) if mod_row is None
                            else (lambda b, i: (layer, mod_row, 0, 0)))
    return pl.BlockSpec(block, (lambda b, i, k: (layer, b, 0, 0)) if mod_row is None
                        else (lambda b, i, k: (layer, mod_row, 0, 0)))


def _qkv_kernel(x_ref, mod_ref, g_ref, w_ref, qn_ref, kn_ref, *rest, rope):
    if rope:
        cos_ref, sa_ref, sb_ref, o_ref, vt_ref = rest
    else:
        o_ref, vt_ref = rest
    d = x_ref.shape[-1]
    mod = mod_ref[0, 0]
    h = _norm_mod(x_ref[0], g_ref[0], mod[:, 0:d], mod[:, d:2 * d]).astype(_BF16)
    for c0, o0, nh, kind, row, rotary in _REGIONS:
        y = jnp.dot(h, w_ref[0, :, c0 * HEAD_DIM:(c0 + nh) * HEAD_DIM], preferred_element_type=_F32)
        for j in range(nh):
            yj = y[:, j * HEAD_DIM:(j + 1) * HEAD_DIM]
            if kind is not None:
                gain = (qn_ref if kind == "q" else kn_ref)[0, row:row + 1, :]
                yj = yj * lax.rsqrt(jnp.mean(yj * yj, axis=-1, keepdims=True) + EPS) * gain
                if rope and rotary:
                    yj = (yj * cos_ref[...]
                          + pltpu.roll(yj, HEAD_DIM - HEAD_DIM // 4, 1) * sa_ref[...]
                          + pltpu.roll(yj, HEAD_DIM // 4, 1) * sb_ref[...])
                if kind == "q":
                    yj = yj * LOGIT2_SCALE
            o_ref[0, o0 + j] = yj.astype(_BF16)
            if o0 == VA0:
                vt_ref[0, j, :HEAD_DIM, :] = yj.T.astype(_BF16)
                vt_ref[0, j, HEAD_DIM:, :] = jnp.ones((VT_ROWS - HEAD_DIM, yj.shape[0]), _BF16)


def _qkv_proj(x, mod, layer, mod_row, gain, w_in, qn, kn, rope_tabs, tm):
    bsz, length, d = x.shape
    d_in = w_in.shape[-1]
    tm = min(tm, length)
    in_specs = [
        pl.BlockSpec((1, tm, d), lambda b, i: (b, i, 0)),
        _mod_spec(mod, layer, mod_row, 2),
        pl.BlockSpec((1, 1, d), lambda b, i: (layer, 0, 0)),
        pl.BlockSpec((1, d, d_in), lambda b, i: (layer, 0, 0), pipeline_mode=pl.Buffered(1)),
        pl.BlockSpec((1,) + qn.shape[1:], lambda b, i: (layer, 0, 0)),
        pl.BlockSpec((1,) + kn.shape[1:], lambda b, i: (layer, 0, 0)),
    ]
    args = [x, mod, gain, w_in, qn, kn]
    if rope_tabs is not None:
        in_specs += [pl.BlockSpec((tm, HEAD_DIM), lambda b, i: (i, 0))] * 3
        args += list(rope_tabs)
    return pl.pallas_call(
        functools.partial(_qkv_kernel, rope=rope_tabs is not None),
        grid=(bsz, length // tm),
        in_specs=in_specs,
        out_specs=[pl.BlockSpec((1, N_HEADS_IN, tm, HEAD_DIM), lambda b, i: (b, 0, i, 0)),
                   pl.BlockSpec((1, KV_A, VT_ROWS, tm), lambda b, i: (b, 0, 0, i))],
        out_shape=[jax.ShapeDtypeStruct((bsz, N_HEADS_IN, length, HEAD_DIM), _BF16),
                   jax.ShapeDtypeStruct((bsz, KV_A, VT_ROWS, length), _BF16)],
        compiler_params=_cparams("parallel", "parallel"),
        name="qkv_proj_rope" if rope_tabs is not None else "qkv_proj_ctx",
    )(*args)


def _head_spec(rows, head0, heads=1):
    return pl.BlockSpec((1, heads, rows, HEAD_DIM), lambda b, h, t: (b, head0 // heads + h, 0, 0))


def _ctx_head_spec(rows, head0, heads=1):
    return pl.BlockSpec((1, heads, rows, HEAD_DIM), lambda b, h, t: (0, head0 // heads + h, b, 0))


def _query_spec(rows, head0, heads):
    return pl.BlockSpec((1, heads, rows, HEAD_DIM), lambda b, h, t: (b, head0 // heads + h, t, 0))


def _attn_a_kernel(bound_ref, q_ref, k_ref, vt_ref, kc_ref, vtc_ref, o_ref, acc_sc, s0_sc, s1_sc,
                   *, tk, layer):
    tq = q_ref.shape[2]
    length = k_ref.shape[2]
    nq = G_A * tq
    n_blocks = length // tk
    q2 = q_ref[0].reshape(nq, HEAD_DIM)

    def qk(kb):
        return _logits2_t(kb, q2)

    def keys(ref, j):
        return ref[0, 0, pl.ds(pl.multiple_of(j * tk, tk), tk), :]

    def values_t(j):
        return vt_ref[0, 0, :, pl.ds(pl.multiple_of(j * tk, tk), tk)]

    bound = bound_ref[layer, 0]

    def shifted_pv(kb, vb):
        p = jnp.exp2(qk(kb) - bound)
        pv = jnp.dot(vb[:HEAD_DIM], p.astype(_BF16), preferred_element_type=_F32)
        return pv, p.sum(axis=0, keepdims=True)

    pv, row_sum = shifted_pv(kc_ref[0, 0], vtc_ref[0, 0])
    acc_sc[:HEAD_DIM, :] = pv

    def shifted_block(j, row_sum):
        pv, block_sum = shifted_pv(keys(k_ref, j), values_t(j))
        acc_sc[:HEAD_DIM, :] += pv
        return row_sum + block_sum

    row_sum = lax.fori_loop(0, n_blocks, shifted_block, row_sum, unroll=2)
    acc_sc[HEAD_DIM:HEAD_DIM + 1, :] = row_sum

    @pl.when(jnp.min(row_sum) < MIN_ROW_SUM)
    def _():
        _attn_a_running_max(qk, keys, values_t, kc_ref, vtc_ref, k_ref, acc_sc, s0_sc, s1_sc, n_blocks, nq)

    o = (acc_sc[:HEAD_DIM, :] / acc_sc[HEAD_DIM:HEAD_DIM + 1, :]).T
    o_ref[0] = jnp.concatenate([o[g * tq:(g + 1) * tq] for g in range(G_A)], axis=1).astype(_BF16)


def _attn_a_running_max(qk, keys, values_t, kc_ref, vtc_ref, k_ref, acc_sc, s0_sc, s1_sc, n_blocks, nq):
    def softmax_pv(s, vb, m_prev):
        m_new = jnp.maximum(m_prev, s.max(axis=0, keepdims=True))
        alpha = jnp.exp2(m_prev - m_new)
        p = jnp.exp2(s - m_new)
        pv = jnp.dot(vb, p.astype(_BF16), preferred_element_type=_F32)
        acc_sc[...] = alpha * acc_sc[...] + pv
        return m_new

    acc_sc[...] = jnp.zeros_like(acc_sc)
    m0 = jnp.full((1, nq), NEG_INF, _F32)
    s0_sc[...] = qk(keys(k_ref, 0))
    m = softmax_pv(qk(kc_ref[0, 0]), vtc_ref[0, 0], m0)

    def pair(j0, m, last):
        s1_sc[...] = qk(keys(k_ref, j0 + 1))
        m = softmax_pv(s0_sc[...], values_t(j0), m)
        if not last:
            s0_sc[...] = qk(keys(k_ref, j0 + 2))
        return softmax_pv(s1_sc[...], values_t(j0 + 1), m)

    m = lax.fori_loop(0, n_blocks // 2 - 1, lambda jj, m: pair(2 * jj, m, False), m)
    pair(n_blocks - 2, m, True)


def _attn_a(qkv, vt, qkv_ctx, vt_ctx, bounds, layer, tq, tk):
    bsz, _, length, _ = qkv.shape
    n_ctx = qkv_ctx.shape[2] // bsz
    tq = min(tq, length)
    tk = min(tk, length // 2)
    assert length % (2 * tk) == 0
    return pl.pallas_call(
        functools.partial(_attn_a_kernel, tk=tk, layer=layer),
        grid=(bsz, KV_A, length // tq),
        in_specs=[pl.BlockSpec(memory_space=pltpu.SMEM),
                  _query_spec(tq, QA0, G_A), _head_spec(length, KA0),
                  pl.BlockSpec((1, 1, VT_ROWS, length), lambda b, h, i: (b, h, 0, 0)),
                  _ctx_head_spec(n_ctx, KA0),
                  pl.BlockSpec((1, 1, VT_ROWS, n_ctx), lambda b, h, i: (0, h, 0, b))],
        out_specs=pl.BlockSpec((1, tq, G_A * HEAD_DIM), lambda b, h, i: (b, i, h)),
        out_shape=jax.ShapeDtypeStruct((bsz, length, H_A * HEAD_DIM), _BF16),
        scratch_shapes=[pltpu.VMEM((VT_ROWS, G_A * tq), _F32),
                        pltpu.VMEM((tk, G_A * tq), _F32), pltpu.VMEM((tk, G_A * tq), _F32)],
        compiler_params=_cparams("parallel", "parallel", "parallel"),
        name="attn_global",
    )(bounds, qkv, qkv, vt, qkv_ctx, vt_ctx)


def _attn_b_kernel(bound_ref, q_ref, k_ref, v_ref, kc_ref, vc_ref, bias_ref, o_ref, *, layer):
    n_rows = k_ref.shape[2] // GRID_W
    tq = NB_QROWS * GRID_W
    nk = NB_KROWS * GRID_W
    tiles = q_ref.shape[2] // tq
    n_tiles = pl.num_programs(2) * tiles
    units = []
    for i in range(tiles):
        t = pl.program_id(2) * tiles + i
        ks = jnp.clip(t * NB_QROWS - NA_ROWS // 2, 0, n_rows - NB_KROWS)
        off = pl.multiple_of(ks * GRID_W, GRID_W)
        variant = jnp.where(t == 0, 0, jnp.where(t == n_tiles - 1, 2, 1))
        for hh in range(NB_HEADS_PER_STEP):
            kvs = [(k_ref[0, hh, pl.ds(off, nk), :], v_ref[0, hh, pl.ds(off, nk), :],
                    bias_ref[0, hh, variant], None),
                   (kc_ref[0, hh], vc_ref[0, hh], None, None)]
            units.append((q_ref[0, hh, i * tq:(i + 1) * tq, :], kvs, None))

    def write(outs):
        for i in range(tiles):
            heads = outs[i * NB_HEADS_PER_STEP:(i + 1) * NB_HEADS_PER_STEP]
            o_ref[0, i * tq:(i + 1) * tq, :] = jnp.concatenate([o.T for o in heads], axis=1).astype(_BF16)

    outs, min_sum = _attend_units_shifted(units, bound_ref[layer, 1])
    write(outs)

    @pl.when(min_sum < MIN_ROW_SUM)
    def _():
        write(_attend_units(units))


def _nb_bias_tables(rpb, n_rows):
    kh = NA_ROWS
    c = np.arange(GRID_W)[:, None, None]
    j = np.arange(2 * NA_COLS - 1)[None, :, None]
    ck = np.arange(GRID_W)[None, None, :]
    col_onehot = (j == ck - c + (NA_COLS - 1)).astype(np.float32)
    cs = np.clip(c - NA_COLS // 2, 0, GRID_W - NA_COLS)
    col_ok = ((ck >= cs) & (ck < cs + NA_COLS))[:, 0, :]
    blocks = jnp.einsum("lhrj,cjk->lhrkc", rpb * LOG2_E, col_onehot, precision=lax.Precision.HIGHEST)
    blocks = jnp.where(col_ok.T, blocks, NEG_INF).astype(_F32)
    masked = jnp.full(blocks.shape[:2] + (GRID_W, GRID_W), NEG_INF, _F32)
    tabs = []
    for r0, ks in ((0, 0), (NB_QROWS, 0), (n_rows - NB_QROWS, n_rows - NB_KROWS)):
        key_rows = []
        for i in range(NB_KROWS):
            row = []
            for a in range(NB_QROWS):
                rq, rk = r0 + a, ks + i
                rs = min(max(rq - kh // 2, 0), n_rows - kh)
                row.append(blocks[:, :, rk - rq + (NA_ROWS - 1)] if rs <= rk < rs + kh else masked)
            key_rows.append(jnp.concatenate(row, axis=-1))
        tabs.append(jnp.concatenate(key_rows, axis=-2))
    return jnp.stack(tabs, axis=2)


def _attn_b(qkv, qkv_ctx, bias_tabs, bounds, layer, tiles):
    bsz, _, length, _ = qkv.shape
    n_ctx = qkv_ctx.shape[2] // bsz
    tq = min(tiles * NB_QROWS * GRID_W, length)
    assert length % tq == 0
    hps = NB_HEADS_PER_STEP
    return pl.pallas_call(
        functools.partial(_attn_b_kernel, layer=layer),
        grid=(bsz, H_B // hps, length // tq),
        in_specs=[pl.BlockSpec(memory_space=pltpu.SMEM),
                  _query_spec(tq, QB0, hps), _head_spec(length, KB0, hps), _head_spec(length, VB0, hps),
                  _ctx_head_spec(n_ctx, KB0, hps), _ctx_head_spec(n_ctx, VB0, hps),
                  pl.BlockSpec((1, hps) + bias_tabs.shape[2:], lambda b, h, t: (layer, h, 0, 0, 0))],
        out_specs=pl.BlockSpec((1, tq, hps * HEAD_DIM), lambda b, h, t: (b, t, h)),
        out_shape=jax.ShapeDtypeStruct((bsz, length, H_B * HEAD_DIM), _BF16),
        compiler_params=_cparams("parallel", "parallel", "parallel"),
        name="attn_neighbourhood",
    )(bounds, qkv, qkv, qkv, qkv_ctx, qkv_ctx, bias_tabs)


def _sink2_row(sink):
    return jnp.concatenate([sink[g:g + 1, :] for g in range(G_C)], axis=1) * LOG2_E


def _attn_c_kernel(bound_ref, q_ref, k_ref, v_ref, kc_ref, vc_ref, sink_ref, o_ref, *, layer):
    t = pl.program_id(2)
    tq = q_ref.shape[2]
    length = k_ref.shape[2]
    band = 3 * WINDOW
    nq = G_C * WINDOW
    kc = kc_ref[0, 0]
    vc = vc_ref[0, 0]
    sink2 = _sink2_row(sink_ref[0, 0])
    rel = (lax.broadcasted_iota(jnp.int32, (band, nq), 1) % WINDOW
           - lax.broadcasted_iota(jnp.int32, (band, nq), 0))
    units = []
    n_sub = tq // WINDOW
    keep_centred = jnp.abs(rel + WINDOW) <= WINDOW
    for n in range(n_sub):
        i0 = t * tq + n * WINDOW
        start = pl.multiple_of(jnp.clip(i0 - WINDOW, 0, length - band), WINDOW)
        kb = k_ref[0, 0, pl.ds(start, band), :]
        vb = v_ref[0, 0, pl.ds(start, band), :]
        q3 = q_ref[0, :, n * WINDOW:(n + 1) * WINDOW, :].reshape(nq, HEAD_DIM)
        keep = keep_centred if 0 < n < n_sub - 1 else jnp.abs(rel + (i0 - start)) <= WINDOW
        units.append((q3, [(kb, vb, None, keep), (kc, vc, None, None)], sink2))

    def write(outs):
        for n, o_t in enumerate(outs):
            o = o_t.T
            o_ref[0, n * WINDOW:(n + 1) * WINDOW, :] = jnp.concatenate(
                [o[g * WINDOW:(g + 1) * WINDOW] for g in range(G_C)], axis=1).astype(_BF16)

    outs, min_sum = _attend_units_shifted(units, bound_ref[layer, 2])
    write(outs)

    @pl.when(min_sum < MIN_ROW_SUM)
    def _():
        write(_attend_units(units))


def _attn_c(qkv, qkv_ctx, sink_tabs, bounds, layer, tq):
    bsz, _, length, _ = qkv.shape
    n_ctx = qkv_ctx.shape[2] // bsz
    tq = min(tq, length)
    return pl.pallas_call(
        functools.partial(_attn_c_kernel, layer=layer),
        grid=(bsz, KV_C, length // tq),
        in_specs=[pl.BlockSpec(memory_space=pltpu.SMEM),
                  _query_spec(tq, QC0, G_C), _head_spec(length, KC0), _head_spec(length, VC0),
                  _ctx_head_spec(n_ctx, KC0), _ctx_head_spec(n_ctx, VC0),
                  pl.BlockSpec((1, 1, G_C, HEAD_DIM), lambda b, h, t: (layer, h, 0, 0))],
        out_specs=pl.BlockSpec((1, tq, G_C * HEAD_DIM), lambda b, h, t: (b, t, h)),
        out_shape=jax.ShapeDtypeStruct((bsz, length, H_C * HEAD_DIM), _BF16),
        compiler_params=_cparams("parallel", "parallel", "parallel"),
        name="attn_window",
    )(bounds, qkv, qkv, qkv, qkv_ctx, qkv_ctx, sink_tabs)


def _attn_ctx_kernel(qkv_ref, sink_ref, ya_ref, yb_ref, yc_ref):
    n = qkv_ref.shape[2]

    def head(u):
        return qkv_ref[0, u]

    def heads(u0, g):
        return qkv_ref[0, u0:u0 + g].reshape(g * n, HEAD_DIM)

    for kv in range(KV_A):
        o = _attend_t(heads(QA0 + kv * G_A, G_A), [(head(KA0 + kv), head(VA0 + kv), None, None)]).T
        for g in range(G_A):
            u = kv * G_A + g
            ya_ref[0, :, u * HEAD_DIM:(u + 1) * HEAD_DIM] = o[g * n:(g + 1) * n].astype(_BF16)
    for hb in range(H_B):
        o = _attend_t(head(QB0 + hb), [(head(KB0 + hb), head(VB0 + hb), None, None)]).T
        yb_ref[0, :, hb * HEAD_DIM:(hb + 1) * HEAD_DIM] = o.astype(_BF16)
    for kv in range(KV_C):
        sink = sink_ref[0, kv]
        sink2 = jnp.concatenate(
            [jnp.broadcast_to(sink[g:g + 1, :1], (1, n)) for g in range(G_C)], axis=1) * LOG2_E
        o = _attend_t(heads(QC0 + kv * G_C, G_C), [(head(KC0 + kv), head(VC0 + kv), None, None)], sink2).T
        for g in range(G_C):
            u = kv * G_C + g
            yc_ref[0, :, u * HEAD_DIM:(u + 1) * HEAD_DIM] = o[g * n:(g + 1) * n].astype(_BF16)


def _attn_ctx(qkv_ctx, sink_tabs, layer, bsz):
    n_heads, rows = qkv_ctx.shape[1:3]
    n = rows // bsz
    widths = (H_A * HEAD_DIM, H_B * HEAD_DIM, H_C * HEAD_DIM)
    return pl.pallas_call(
        _attn_ctx_kernel,
        grid=(bsz,),
        in_specs=[pl.BlockSpec((1, n_heads, n, HEAD_DIM), lambda b: (0, 0, b, 0)),
                  pl.BlockSpec((1,) + sink_tabs.shape[1:], lambda b: (layer, 0, 0, 0))],
        out_specs=[pl.BlockSpec((1, n, w), lambda b: (0, b, 0)) for w in widths],
        out_shape=[jax.ShapeDtypeStruct((1, rows, w), _BF16) for w in widths],
        compiler_params=_cparams("parallel"),
        name="attn_ctx",
    )(qkv_ctx, sink_tabs)


def _outproj_kernel(ya_ref, yb_ref, yc_ref, x_ref, mod_ref, g_ref, w_ref, xo_ref, h_ref):
    tm, d = x_ref.shape[1:]
    mod = mod_ref[0, 0]
    chunks = [slice(m0, min(m0 + OUTPROJ_ROW_CHUNK, tm)) for m0 in range(0, tm, OUTPROJ_ROW_CHUNK)]

    def project(rows):
        acc = None
        r0 = 0
        for y_ref in (ya_ref, yb_ref, yc_ref):
            r1 = r0 + y_ref.shape[-1]
            part = jnp.dot(y_ref[0, rows, :], w_ref[0, r0:r1, :], preferred_element_type=_F32)
            acc = part if acc is None else acc + part
            r0 = r1
        return acc

    acc_next = project(chunks[0])
    for i, rows in enumerate(chunks):
        acc = acc_next
        if i + 1 < len(chunks):
            acc_next = project(chunks[i + 1])
        xn = x_ref[0, rows, :] + mod[:, 2 * d:3 * d] * acc
        xo_ref[0, rows, :] = xn
        h_ref[0, rows, :] = _norm_mod(xn, g_ref[0], mod[:, 3 * d:4 * d],
                                      mod[:, 4 * d:5 * d]).astype(_BF16)


def _outproj(ya, yb, yc, x, mod, layer, mod_row, gain, w_out, tm):
    bsz, length, d = x.shape
    tm = min(tm, length)
    row = lambda b, i: (b, i, 0)
    return pl.pallas_call(
        _outproj_kernel,
        grid=(bsz, length // tm),
        in_specs=[pl.BlockSpec((1, tm, y.shape[-1]), row) for y in (ya, yb, yc)] + [
            pl.BlockSpec((1, tm, d), row),
            _mod_spec(mod, layer, mod_row, 2),
            pl.BlockSpec((1, 1, d), lambda b, i: (layer, 0, 0)),
            pl.BlockSpec((1,) + w_out.shape[1:], lambda b, i: (layer, 0, 0), pipeline_mode=pl.Buffered(1)),
        ],
        out_specs=[pl.BlockSpec((1, tm, d), row), pl.BlockSpec((1, tm, d), row)],
        out_shape=[jax.ShapeDtypeStruct((bsz, length, d), _F32),
                   jax.ShapeDtypeStruct((bsz, length, d), _BF16)],
        compiler_params=_cparams("parallel", "parallel"),
        name="out_proj_residual",
    )(ya, yb, yc, x, mod, gain, w_out)


def _mlp_kernel(h_ref, x_ref, mod_ref, wu_ref, wd_ref, o_ref):
    k = pl.program_id(2)
    d = x_ref.shape[-1]
    @pl.when(k == 0)
    def _():
        o_ref[...] = jnp.zeros_like(o_ref)

    u = jnp.dot(h_ref[0], wu_ref[0], preferred_element_type=_F32)
    act = jnp.square(jnp.maximum(u, 0.0)).astype(_BF16)
    o_ref[0] += jnp.dot(act, wd_ref[0], preferred_element_type=_F32)

    @pl.when(k == pl.num_programs(2) - 1)
    def _():
        o_ref[0] = x_ref[0] + mod_ref[0, 0, :, 5 * d:6 * d] * o_ref[0]


def _mlp(h, x, mod, layer, mod_row, w_up, w_down, tm, tf):
    bsz, length, d = x.shape
    d_ff = w_up.shape[-1]
    tm = min(tm, length)
    tf = min(tf, d_ff)
    row = lambda b, i, k: (b, i, 0)
    return pl.pallas_call(
        _mlp_kernel,
        grid=(bsz, length // tm, d_ff // tf),
        in_specs=[pl.BlockSpec((1, tm, d), row),
                  pl.BlockSpec((1, tm, d), row),
                  _mod_spec(mod, layer, mod_row, 3),
                  pl.BlockSpec((1, d, tf), lambda b, i, k: (layer, 0, k)),
                  pl.BlockSpec((1, tf, d), lambda b, i, k: (layer, k, 0))],
        out_specs=pl.BlockSpec((1, tm, d), row),
        out_shape=jax.ShapeDtypeStruct((bsz, length, d), _F32),
        compiler_params=_cparams("parallel", "parallel", "arbitrary"),
        name="mlp_sq_relu",
    )(h, x, mod, w_up, w_down)


def _logit_bounds(q_norm, k_norm, rel_pos_bias, sink_logits):
    gq = jnp.max(jnp.abs(q_norm), axis=-1)
    gk = jnp.max(jnp.abs(k_norm), axis=-1)
    qk_bound = HEAD_DIM * LOGIT2_SCALE * BOUND_MARGIN * gq * gk + BOUND_MARGIN - 1.0
    bias_max = jnp.maximum(jnp.max(rel_pos_bias, axis=(1, 2, 3)), 0.0) * LOG2_E
    sink_max = jnp.max(sink_logits, axis=-1) * LOG2_E
    return jnp.stack([qk_bound[:, 0], qk_bound[:, 1] + bias_max,
                      jnp.maximum(qk_bound[:, 2], sink_max)], axis=-1).astype(_F32)


def _rope_tables(length):
    t = np.arange(length)
    nf = HEAD_DIM // 4
    inv = ROPE_THETA ** (-jnp.arange(nf, dtype=_F32) / nf)
    ar = jnp.asarray(t // GRID_W, _F32)[:, None] * inv
    ac = jnp.asarray(t % GRID_W, _F32)[:, None] * inv
    ang = jnp.concatenate([ar, ar, ac, ac], axis=-1)
    cos, sin = jnp.cos(ang), jnp.sin(ang)
    first = (np.arange(HEAD_DIM) // nf) % 2 == 0
    return cos, jnp.where(first, -sin, 0.0), jnp.where(first, 0.0, sin)


def kernel(x, c, ctx, c_ctx, w_mod, b_mod, norm_attn, norm_mlp, w_in, q_norm, k_norm,
           rel_pos_bias, sink_logits, w_out, w_up, w_down):
    bsz, length, d = x.shape
    depth = w_mod.shape[0]
    assert length % (NB_QROWS * GRID_W) == 0 and length // GRID_W >= NB_KROWS
    assert w_in.shape[-1] == N_HEADS_IN * HEAD_DIM and w_out.shape[1] == N_HEADS_OUT * HEAD_DIM

    mod_rows = 8
    c_rows = jnp.zeros((mod_rows, d), _F32).at[:bsz].set(c).at[bsz].set(c_ctx)
    mod = _modulation(c_rows, w_mod, b_mod).reshape(depth, mod_rows, 1, N_MOD * d)

    w_in_b, w_out_b = w_in.astype(_BF16), w_out.astype(_BF16)
    w_up_b, w_down_b = w_up.astype(_BF16), w_down.astype(_BF16)
    g_attn = norm_attn.reshape(depth, 1, d)
    g_mlp = norm_mlp.reshape(depth, 1, d)
    rope_tabs = _rope_tables(length)
    bounds = _logit_bounds(q_norm, k_norm, rel_pos_bias, sink_logits)
    bias_tabs = _nb_bias_tables(rel_pos_bias, length // GRID_W)
    sink_tabs = jnp.broadcast_to(
        sink_logits.reshape(depth, KV_C, G_C, 1), (depth, KV_C, G_C, HEAD_DIM)).astype(_F32)

    cs = ctx.reshape(1, bsz * ctx.shape[1], d)
    for l in range(depth):
        last = l == depth - 1
        qkv, vt = _qkv_proj(x, mod, l, None, g_attn, w_in_b, q_norm, k_norm, rope_tabs, tm=256)
        qkv_c, vt_c = _qkv_proj(cs, mod, l, bsz, g_attn, w_in_b, q_norm, k_norm, None, tm=512)
        ya = _attn_a(qkv, vt, qkv_c, vt_c, bounds, l, tq=1024, tk=1024)
        yb = _attn_b(qkv, qkv_c, bias_tabs, bounds, l, tiles=8)
        yc = _attn_c(qkv, qkv_c, sink_tabs, bounds, l, tq=4096)
        x, h = _outproj(ya, yb, yc, x, mod, l, None, g_mlp, w_out_b, tm=512)
        x = _mlp(h, x, mod, l, None, w_up_b, w_down_b, tm=512, tf=1024)
        if not last:
            ya_c, yb_c, yc_c = _attn_ctx(qkv_c, sink_tabs, l, bsz)
            cs, hc = _outproj(ya_c, yb_c, yc_c, cs, mod, l, bsz, g_mlp, w_out_b, tm=512)
            cs = _mlp(hc, cs, mod, l, bsz, w_up_b, w_down_b, tm=512, tf=1024)
    return x
```

```python
import functools

import numpy as np
import jax
import jax.numpy as jnp
from jax import lax
from jax.experimental import pallas as pl
from jax.experimental.pallas import tpu as pltpu

HEAD_DIM = 128
H_A, KV_A = 4, 2
H_B = 6
H_C, KV_C = 6, 2
G_A = H_A // KV_A
G_C = H_C // KV_C
GRID_W = 64
NA_ROWS = 8
NA_COLS = 16
WINDOW = 128
ROPE_THETA = 10000.0
EPS = 1e-6
N_MOD = 6
ATTN_SCALE = HEAD_DIM ** -0.5
NEG_INF = -1e30
LOG2_E = 1.4426950408889634
LOGIT2_SCALE = ATTN_SCALE * LOG2_E
BOUND_MARGIN = 1.02
MIN_ROW_SUM = 2.0 ** -64

_IN_QA, _IN_KA, _IN_VA = 0, H_A, H_A + KV_A
_IN_QB = _IN_VA + KV_A
_IN_KB, _IN_VB = _IN_QB + H_B, _IN_QB + 2 * H_B
_IN_QC = _IN_VB + H_B
_IN_KC, _IN_VC = _IN_QC + H_C, _IN_QC + H_C + KV_C
N_HEADS_IN = _IN_VC + KV_C
N_HEADS_OUT = H_A + H_B + H_C

QC0 = 0
QA0 = QC0 + H_C
KA0 = QA0 + H_A
VA0 = KA0 + KV_A
QB0 = VA0 + KV_A
KB0 = QB0 + H_B
VB0 = KB0 + H_B
KC0 = VB0 + H_B
VC0 = KC0 + KV_C
assert QA0 % G_A == 0 and QC0 % G_C == 0 and VC0 + KV_C == N_HEADS_IN

_REGIONS = (
    (_IN_QA, QA0, H_A, "q", 0, True), (_IN_KA, KA0, KV_A, "k", 0, True), (_IN_VA, VA0, KV_A, None, 0, False),
    (_IN_QB, QB0, H_B, "q", 1, False), (_IN_KB, KB0, H_B, "k", 1, False), (_IN_VB, VB0, H_B, None, 0, False),
    (_IN_QC, QC0, H_C, "q", 2, True), (_IN_KC, KC0, KV_C, "k", 2, True), (_IN_VC, VC0, KV_C, None, 0, False),
)

VT_ROWS = HEAD_DIM + 16
NB_QROWS = 4
NB_KROWS = NB_QROWS + NA_ROWS - 1
NB_HEADS_PER_STEP = 2
OUTPROJ_ROW_CHUNK = 256
VMEM_LIMIT_BYTES = 56 * 1024 * 1024

_BF16 = jnp.bfloat16
_F32 = jnp.float32
_NT = (((1,), (1,)), ((), ()))
_TN = (((0,), (0,)), ((), ()))


def _cparams(*sem):
    return pltpu.CompilerParams(dimension_semantics=sem, vmem_limit_bytes=VMEM_LIMIT_BYTES)


def _norm_mod(x, gain, shift, scale):
    y = x * lax.rsqrt(jnp.mean(x * x, axis=-1, keepdims=True) + EPS)
    return (y * gain) * (1.0 + scale) + shift


def _logits2_t(k, q):
    return lax.dot_general(k, q, _NT, preferred_element_type=_F32)


def _attend_logits(q, kvs):
    zs = []
    for k, _, bias2, keep in kvs:
        z = _logits2_t(k, q)
        if bias2 is not None:
            z = z + bias2
        if keep is not None:
            z = jnp.where(keep, z, NEG_INF)
        zs.append(z)
    return zs


def _attend_t(q, kvs, sink2_row=None, zs=None):
    if zs is None:
        zs = _attend_logits(q, kvs)
    m = zs[0].max(axis=0, keepdims=True)
    for z in zs[1:]:
        m = jnp.maximum(m, z.max(axis=0, keepdims=True))
    if sink2_row is not None:
        m = jnp.maximum(m, sink2_row)
    l = None
    acc = None
    for z, (_, v, _, _) in zip(zs, kvs):
        p = jnp.exp2(z - m)
        ls = p.sum(axis=0, keepdims=True)
        pv = lax.dot_general(v, p.astype(_BF16), _TN, preferred_element_type=_F32)
        l = ls if l is None else l + ls
        acc = pv if acc is None else acc + pv
    if sink2_row is not None:
        l = l + jnp.exp2(sink2_row - m)
    return acc / l


def _attend_units(units):
    outs = []
    zs_next = _attend_logits(*units[0][:2])
    for u, (q, kvs, sink2) in enumerate(units):
        zs = zs_next
        if u + 1 < len(units):
            zs_next = _attend_logits(*units[u + 1][:2])
        outs.append(_attend_t(q, kvs, sink2, zs))
    return outs


def _attend_units_shifted(units, bound):
    outs = []
    min_sum = None
    zs_next = _attend_logits(*units[0][:2])
    for u, (q, kvs, sink2) in enumerate(units):
        zs = zs_next
        if u + 1 < len(units):
            zs_next = _attend_logits(*units[u + 1][:2])
        l = None
        acc = None
        for z, (_, v, _, _) in zip(zs, kvs):
            p = jnp.exp2(z if bound is None else z - bound)
            ls = p.sum(axis=0, keepdims=True)
            pv = lax.dot_general(v, p.astype(_BF16), _TN, preferred_element_type=_F32)
            l = ls if l is None else l + ls
            acc = pv if acc is None else acc + pv
        if sink2 is not None:
            l = l + jnp.exp2(sink2 if bound is None else sink2 - bound)
        outs.append(acc / l)
        min_sum = l if min_sum is None else jnp.minimum(min_sum, l)
    return outs, jnp.min(min_sum)


def _mod_kernel(c_ref, w_ref, b_ref, o_ref):
    c = c_ref[...]
    cond = c * jax.nn.sigmoid(c)
    o_ref[0] = jnp.dot(cond.astype(_BF16), w_ref[0].astype(_BF16),
                       preferred_element_type=_F32) + b_ref[0]


def _modulation(c_rows, w_mod, b_mod):
    depth, d, n = w_mod.shape
    tn = next(t for t in (1024, 512, 256, 128, n) if n % t == 0)
    rows = c_rows.shape[0]
    return pl.pallas_call(
        _mod_kernel,
        grid=(depth, n // tn),
        in_specs=[pl.BlockSpec((rows, d), lambda l, j: (0, 0)),
                  pl.BlockSpec((1, d, tn), lambda l, j: (l, 0, j)),
                  pl.BlockSpec((1, 1, tn), lambda l, j: (l, 0, j))],
        out_specs=pl.BlockSpec((1, rows, tn), lambda l, j: (l, 0, j)),
        out_shape=jax.ShapeDtypeStruct((depth, rows, n), _F32),
        compiler_params=_cparams("parallel", "parallel"),
        name="adaln_modulation",
    )(c_rows, w_mod, b_mod.reshape(depth, 1, n))


def _mod_spec(mod, layer, mod_row, n_grid):
    block = (1, 1, 1, mod.shape[-1])
    if n_grid == 2:
        return pl.BlockSpec(block, (lambda b, i: (layer, b, 0, 0)) if mod_row is None
                            else (lambda b, i: (layer, mod_row, 0, 0)))
    return pl.BlockSpec(block, (lambda b, i, k: (layer, b, 0, 0)) if mod_row is None
                        else (lambda b, i, k: (layer, mod_row, 0, 0)))


def _qkv_kernel(x_ref, mod_ref, g_ref, w_ref, qn_ref, kn_ref, *rest, rope):
    if rope:
        cos_ref, sa_ref, sb_ref, o_ref, vt_ref = rest
    else:
        o_ref, vt_ref = rest
    d = x_ref.shape[-1]
    mod = mod_ref[0, 0]
    h = _norm_mod(x_ref[0], g_ref[0], mod[:, 0:d], mod[:, d:2 * d]).astype(_BF16)
    for c0, o0, nh, kind, row, rotary in _REGIONS:
        y = jnp.dot(h, w_ref[0, :, c0 * HEAD_DIM:(c0 + nh) * HEAD_DIM], preferred_element_type=_F32)
        for j in range(nh):
            yj = y[:, j * HEAD_DIM:(j + 1) * HEAD_DIM]
            if kind is not None:
                gain = (qn_ref if kind == "q" else kn_ref)[0, row:row + 1, :]
                yj = yj * lax.rsqrt(jnp.mean(yj * yj, axis=-1, keepdims=True) + EPS) * gain
                if rope and rotary:
                    yj = (yj * cos_ref[...]
                          + pltpu.roll(yj, HEAD_DIM - HEAD_DIM // 4, 1) * sa_ref[...]
                          + pltpu.roll(yj, HEAD_DIM // 4, 1) * sb_ref[...])
                if kind == "q":
                    yj = yj * LOGIT2_SCALE
            o_ref[0, o0 + j] = yj.astype(_BF16)
            if o0 == VA0:
                vt_ref[0, j, :HEAD_DIM, :] = yj.T.astype(_BF16)
                vt_ref[0, j, HEAD_DIM:, :] = jnp.ones((VT_ROWS - HEAD_DIM, yj.shape[0]), _BF16)


def _qkv_proj(x, mod, layer, mod_row, gain, w_in, qn, kn, rope_tabs, tm):
    bsz, length, d = x.shape
    d_in = w_in.shape[-1]
    tm = min(tm, length)
    in_specs = [
        pl.BlockSpec((1, tm, d), lambda b, i: (b, i, 0)),
        _mod_spec(mod, layer, mod_row, 2),
        pl.BlockSpec((1, 1, d), lambda b, i: (layer, 0, 0)),
        pl.BlockSpec((1, d, d_in), lambda b, i: (layer, 0, 0), pipeline_mode=pl.Buffered(1)),
        pl.BlockSpec((1,) + qn.shape[1:], lambda b, i: (layer, 0, 0)),
        pl.BlockSpec((1,) + kn.shape[1:], lambda b, i: (layer, 0, 0)),
    ]
    args = [x, mod, gain, w_in, qn, kn]
    if rope_tabs is not None:
        in_specs += [pl.BlockSpec((tm, HEAD_DIM), lambda b, i: (i, 0))] * 3
        args += list(rope_tabs)
    return pl.pallas_call(
        functools.partial(_qkv_kernel, rope=rope_tabs is not None),
        grid=(bsz, length // tm),
        in_specs=in_specs,
        out_specs=[pl.BlockSpec((1, N_HEADS_IN, tm, HEAD_DIM), lambda b, i: (b, 0, i, 0)),
                   pl.BlockSpec((1, KV_A, VT_ROWS, tm), lambda b, i: (b, 0, 0, i))],
        out_shape=[jax.ShapeDtypeStruct((bsz, N_HEADS_IN, length, HEAD_DIM), _BF16),
                   jax.ShapeDtypeStruct((bsz, KV_A, VT_ROWS, length), _BF16)],
        compiler_params=_cparams("parallel", "parallel"),
        name="qkv_proj_rope" if rope_tabs is not None else "qkv_proj_ctx",
    )(*args)


def _head_spec(rows, head0, heads=1):
    return pl.BlockSpec((1, heads, rows, HEAD_DIM), lambda b, h, t: (b, head0 // heads + h, 0, 0))


def _ctx_head_spec(rows, head0, heads=1):
    return pl.BlockSpec((1, heads, rows, HEAD_DIM), lambda b, h, t: (0, head0 // heads + h, b, 0))


def _query_spec(rows, head0, heads):
    return pl.BlockSpec((1, heads, rows, HEAD_DIM), lambda b, h, t: (b, head0 // heads + h, t, 0))


def _attn_a_kernel(bound_ref, q_ref, k_ref, vt_ref, kc_ref, vtc_ref, o_ref, acc_sc, s0_sc, s1_sc,
                   *, tk, layer):
    tq = q_ref.shape[2]
    length = k_ref.shape[2]
    nq = G_A * tq
    n_blocks = length // tk
    q2 = q_ref[0].reshape(nq, HEAD_DIM)

    def qk(kb):
        return _logits2_t(kb, q2)

    def keys(ref, j):
        return ref[0, 0, pl.ds(pl.multiple_of(j * tk, tk), tk), :]

    def values_t(j):
        return vt_ref[0, 0, :, pl.ds(pl.multiple_of(j * tk, tk), tk)]

    bound = bound_ref[layer, 0]

    def shifted_pv(kb, vb):
        p = jnp.exp2(qk(kb) - bound)
        pv = jnp.dot(vb[:HEAD_DIM], p.astype(_BF16), preferred_element_type=_F32)
        return pv, p.sum(axis=0, keepdims=True)

    pv, row_sum = shifted_pv(kc_ref[0, 0], vtc_ref[0, 0])
    acc_sc[:HEAD_DIM, :] = pv

    def shifted_block(j, row_sum):
        pv, block_sum = shifted_pv(keys(k_ref, j), values_t(j))
        acc_sc[:HEAD_DIM, :] += pv
        return row_sum + block_sum

    row_sum = lax.fori_loop(0, n_blocks, shifted_block, row_sum, unroll=2)
    acc_sc[HEAD_DIM:HEAD_DIM + 1, :] = row_sum

    @pl.when(jnp.min(row_sum) < MIN_ROW_SUM)
    def _():
        _attn_a_running_max(qk, keys, values_t, kc_ref, vtc_ref, k_ref, acc_sc, s0_sc, s1_sc, n_blocks, nq)

    o = (acc_sc[:HEAD_DIM, :] / acc_sc[HEAD_DIM:HEAD_DIM + 1, :]).T
    o_ref[0] = jnp.concatenate([o[g * tq:(g + 1) * tq] for g in range(G_A)], axis=1).astype(_BF16)


def _attn_a_running_max(qk, keys, values_t, kc_ref, vtc_ref, k_ref, acc_sc, s0_sc, s1_sc, n_blocks, nq):
    def softmax_pv(s, vb, m_prev):
        m_new = jnp.maximum(m_prev, s.max(axis=0, keepdims=True))
        alpha = jnp.exp2(m_prev - m_new)
        p = jnp.exp2(s - m_new)
        pv = jnp.dot(vb, p.astype(_BF16), preferred_element_type=_F32)
        acc_sc[...] = alpha * acc_sc[...] + pv
        return m_new

    acc_sc[...] = jnp.zeros_like(acc_sc)
    m0 = jnp.full((1, nq), NEG_INF, _F32)
    s0_sc[...] = qk(keys(k_ref, 0))
    m = softmax_pv(qk(kc_ref[0, 0]), vtc_ref[0, 0], m0)

    def pair(j0, m, last):
        s1_sc[...] = qk(keys(k_ref, j0 + 1))
        m = softmax_pv(s0_sc[...], values_t(j0), m)
        if not last:
            s0_sc[...] = qk(keys(k_ref, j0 + 2))
        return softmax_pv(s1_sc[...], values_t(j0 + 1), m)

    m = lax.fori_loop(0, n_blocks // 2 - 1, lambda jj, m: pair(2 * jj, m, False), m)
    pair(n_blocks - 2, m, True)


def _attn_a(qkv, vt, qkv_ctx, vt_ctx, bounds, layer, tq, tk):
    bsz, _, length, _ = qkv.shape
    n_ctx = qkv_ctx.shape[2] // bsz
    tq = min(tq, length)
    tk = min(tk, length // 2)
    assert length % (2 * tk) == 0
    return pl.pallas_call(
        functools.partial(_attn_a_kernel, tk=tk, layer=layer),
        grid=(bsz, KV_A, length // tq),
        in_specs=[pl.BlockSpec(memory_space=pltpu.SMEM),
                  _query_spec(tq, QA0, G_A), _head_spec(length, KA0),
                  pl.BlockSpec((1, 1, VT_ROWS, length), lambda b, h, i: (b, h, 0, 0)),
                  _ctx_head_spec(n_ctx, KA0),
                  pl.BlockSpec((1, 1, VT_ROWS, n_ctx), lambda b, h, i: (0, h, 0, b))],
        out_specs=pl.BlockSpec((1, tq, G_A * HEAD_DIM), lambda b, h, i: (b, i, h)),
        out_shape=jax.ShapeDtypeStruct((bsz, length, H_A * HEAD_DIM), _BF16),
        scratch_shapes=[pltpu.VMEM((VT_ROWS, G_A * tq), _F32),
                        pltpu.VMEM((tk, G_A * tq), _F32), pltpu.VMEM((tk, G_A * tq), _F32)],
        compiler_params=_cparams("parallel", "parallel", "parallel"),
        name="attn_global",
    )(bounds, qkv, qkv, vt, qkv_ctx, vt_ctx)


def _attn_b_kernel(bound_ref, q_ref, k_ref, v_ref, kc_ref, vc_ref, bias_ref, o_ref, *, layer):
    n_rows = k_ref.shape[2] // GRID_W
    tq = NB_QROWS * GRID_W
    nk = NB_KROWS * GRID_W
    tiles = q_ref.shape[2] // tq
    n_tiles = pl.num_programs(2) * tiles
    bound = bound_ref[layer, 1]
    units = []
    for i in range(tiles):
        t = pl.program_id(2) * tiles + i
        ks = jnp.clip(t * NB_QROWS - NA_ROWS // 2, 0, n_rows - NB_KROWS)
        off = pl.multiple_of(ks * GRID_W, GRID_W)
        variant = jnp.where(t == 0, 0, jnp.where(t == n_tiles - 1, 2, 1))
        for hh in range(NB_HEADS_PER_STEP):
            kvs = [(k_ref[0, hh, pl.ds(off, nk), :], v_ref[0, hh, pl.ds(off, nk), :],
                    bias_ref[0, hh, variant], None),
                   (kc_ref[0, hh], vc_ref[0, hh], -bound, None)]
            units.append((q_ref[0, hh, i * tq:(i + 1) * tq, :], kvs, None))

    def write(outs):
        for i in range(tiles):
            heads = outs[i * NB_HEADS_PER_STEP:(i + 1) * NB_HEADS_PER_STEP]
            o_ref[0, i * tq:(i + 1) * tq, :] = jnp.concatenate([o.T for o in heads], axis=1).astype(_BF16)

    outs, min_sum = _attend_units_shifted(units, None)
    write(outs)

    @pl.when(min_sum < MIN_ROW_SUM)
    def _():
        write(_attend_units(units))


def _nb_bias_tables(rpb, n_rows, shift):
    kh = NA_ROWS
    c = np.arange(GRID_W)[:, None, None]
    j = np.arange(2 * NA_COLS - 1)[None, :, None]
    ck = np.arange(GRID_W)[None, None, :]
    col_onehot = (j == ck - c + (NA_COLS - 1)).astype(np.float32)
    cs = np.clip(c - NA_COLS // 2, 0, GRID_W - NA_COLS)
    col_ok = ((ck >= cs) & (ck < cs + NA_COLS))[:, 0, :]
    blocks = jnp.einsum("lhrj,cjk->lhrkc", rpb * LOG2_E, col_onehot, precision=lax.Precision.HIGHEST)
    blocks = blocks - shift[:, None, None, None, None]
    blocks = jnp.where(col_ok.T, blocks, NEG_INF).astype(_F32)
    masked = jnp.full(blocks.shape[:2] + (GRID_W, GRID_W), NEG_INF, _F32)
    tabs = []
    for r0, ks in ((0, 0), (NB_QROWS, 0), (n_rows - NB_QROWS, n_rows - NB_KROWS)):
        key_rows = []
        for i in range(NB_KROWS):
            row = []
            for a in range(NB_QROWS):
                rq, rk = r0 + a, ks + i
                rs = min(max(rq - kh // 2, 0), n_rows - kh)
                row.append(blocks[:, :, rk - rq + (NA_ROWS - 1)] if rs <= rk < rs + kh else masked)
            key_rows.append(jnp.concatenate(row, axis=-1))
        tabs.append(jnp.concatenate(key_rows, axis=-2))
    return jnp.stack(tabs, axis=2)


def _attn_b(qkv, qkv_ctx, bias_tabs, bounds, layer, tiles):
    bsz, _, length, _ = qkv.shape
    n_ctx = qkv_ctx.shape[2] // bsz
    tq = min(tiles * NB_QROWS * GRID_W, length)
    assert length % tq == 0
    hps = NB_HEADS_PER_STEP
    return pl.pallas_call(
        functools.partial(_attn_b_kernel, layer=layer),
        grid=(bsz, H_B // hps, length // tq),
        in_specs=[pl.BlockSpec(memory_space=pltpu.SMEM),
                  _query_spec(tq, QB0, hps), _head_spec(length, KB0, hps), _head_spec(length, VB0, hps),
                  _ctx_head_spec(n_ctx, KB0, hps), _ctx_head_spec(n_ctx, VB0, hps),
                  pl.BlockSpec((1, hps) + bias_tabs.shape[2:], lambda b, h, t: (layer, h, 0, 0, 0))],
        out_specs=pl.BlockSpec((1, tq, hps * HEAD_DIM), lambda b, h, t: (b, t, h)),
        out_shape=jax.ShapeDtypeStruct((bsz, length, H_B * HEAD_DIM), _BF16),
        compiler_params=_cparams("parallel", "parallel", "parallel"),
        name="attn_neighbourhood",
    )(bounds, qkv, qkv, qkv, qkv_ctx, qkv_ctx, bias_tabs)


def _sink2_row(sink):
    return jnp.concatenate([sink[g:g + 1, :] for g in range(G_C)], axis=1) * LOG2_E


def _attn_c_kernel(bound_ref, q_ref, k_ref, v_ref, kc_ref, vc_ref, sink_ref, o_ref, *, layer):
    t = pl.program_id(2)
    tq = q_ref.shape[2]
    length = k_ref.shape[2]
    band = 3 * WINDOW
    nq = G_C * WINDOW
    kc = kc_ref[0, 0]
    vc = vc_ref[0, 0]
    sink2 = _sink2_row(sink_ref[0, 0])
    rel = (lax.broadcasted_iota(jnp.int32, (band, nq), 1) % WINDOW
           - lax.broadcasted_iota(jnp.int32, (band, nq), 0))
    units = []
    n_sub = tq // WINDOW
    keep_centred = jnp.abs(rel + WINDOW) <= WINDOW
    for n in range(n_sub):
        i0 = t * tq + n * WINDOW
        start = pl.multiple_of(jnp.clip(i0 - WINDOW, 0, length - band), WINDOW)
        kb = k_ref[0, 0, pl.ds(start, band), :]
        vb = v_ref[0, 0, pl.ds(start, band), :]
        q3 = q_ref[0, :, n * WINDOW:(n + 1) * WINDOW, :].reshape(nq, HEAD_DIM)
        keep = keep_centred if 0 < n < n_sub - 1 else jnp.abs(rel + (i0 - start)) <= WINDOW
        units.append((q3, [(kb, vb, None, keep), (kc, vc, None, None)], sink2))

    def write(outs):
        for n, o_t in enumerate(outs):
            o = o_t.T
            o_ref[0, n * WINDOW:(n + 1) * WINDOW, :] = jnp.concatenate(
                [o[g * WINDOW:(g + 1) * WINDOW] for g in range(G_C)], axis=1).astype(_BF16)

    outs, min_sum = _attend_units_shifted(units, bound_ref[layer, 2])
    write(outs)

    @pl.when(min_sum < MIN_ROW_SUM)
    def _():
        write(_attend_units(units))


def _attn_c(qkv, qkv_ctx, sink_tabs, bounds, layer, tq):
    bsz, _, length, _ = qkv.shape
    n_ctx = qkv_ctx.shape[2] // bsz
    tq = min(tq, length)
    return pl.pallas_call(
        functools.partial(_attn_c_kernel, layer=layer),
        grid=(bsz, KV_C, length // tq),
        in_specs=[pl.BlockSpec(memory_space=pltpu.SMEM),
                  _query_spec(tq, QC0, G_C), _head_spec(length, KC0), _head_spec(length, VC0),
                  _ctx_head_spec(n_ctx, KC0), _ctx_head_spec(n_ctx, VC0),
                  pl.BlockSpec((1, 1, G_C, HEAD_DIM), lambda b, h, t: (layer, h, 0, 0))],
        out_specs=pl.BlockSpec((1, tq, G_C * HEAD_DIM), lambda b, h, t: (b, t, h)),
        out_shape=jax.ShapeDtypeStruct((bsz, length, H_C * HEAD_DIM), _BF16),
        compiler_params=_cparams("parallel", "parallel", "parallel"),
        name="attn_window",
    )(bounds, qkv, qkv, qkv, qkv_ctx, qkv_ctx, sink_tabs)


def _attn_ctx_kernel(qkv_ref, sink_ref, ya_ref, yb_ref, yc_ref):
    n = qkv_ref.shape[2]

    def head(u):
        return qkv_ref[0, u]

    def heads(u0, g):
        return qkv_ref[0, u0:u0 + g].reshape(g * n, HEAD_DIM)

    for kv in range(KV_A):
        o = _attend_t(heads(QA0 + kv * G_A, G_A), [(head(KA0 + kv), head(VA0 + kv), None, None)]).T
        for g in range(G_A):
            u = kv * G_A + g
            ya_ref[0, :, u * HEAD_DIM:(u + 1) * HEAD_DIM] = o[g * n:(g + 1) * n].astype(_BF16)
    for hb in range(H_B):
        o = _attend_t(head(QB0 + hb), [(head(KB0 + hb), head(VB0 + hb), None, None)]).T
        yb_ref[0, :, hb * HEAD_DIM:(hb + 1) * HEAD_DIM] = o.astype(_BF16)
    for kv in range(KV_C):
        sink = sink_ref[0, kv]
        sink2 = jnp.concatenate(
            [jnp.broadcast_to(sink[g:g + 1, :1], (1, n)) for g in range(G_C)], axis=1) * LOG2_E
        o = _attend_t(heads(QC0 + kv * G_C, G_C), [(head(KC0 + kv), head(VC0 + kv), None, None)], sink2).T
        for g in range(G_C):
            u = kv * G_C + g
            yc_ref[0, :, u * HEAD_DIM:(u + 1) * HEAD_DIM] = o[g * n:(g + 1) * n].astype(_BF16)


def _attn_ctx(qkv_ctx, sink_tabs, layer, bsz):
    n_heads, rows = qkv_ctx.shape[1:3]
    n = rows // bsz
    widths = (H_A * HEAD_DIM, H_B * HEAD_DIM, H_C * HEAD_DIM)
    return pl.pallas_call(
        _attn_ctx_kernel,
        grid=(bsz,),
        in_specs=[pl.BlockSpec((1, n_heads, n, HEAD_DIM), lambda b: (0, 0, b, 0)),
                  pl.BlockSpec((1,) + sink_tabs.shape[1:], lambda b: (layer, 0, 0, 0))],
        out_specs=[pl.BlockSpec((1, n, w), lambda b: (0, b, 0)) for w in widths],
        out_shape=[jax.ShapeDtypeStruct((1, rows, w), _BF16) for w in widths],
        compiler_params=_cparams("parallel"),
        name="attn_ctx",
    )(qkv_ctx, sink_tabs)


def _outproj_kernel(ya_ref, yb_ref, yc_ref, x_ref, mod_ref, g_ref, w_ref, xo_ref, h_ref):
    tm, d = x_ref.shape[1:]
    mod = mod_ref[0, 0]
    chunks = [slice(m0, min(m0 + OUTPROJ_ROW_CHUNK, tm)) for m0 in range(0, tm, OUTPROJ_ROW_CHUNK)]

    def project(rows):
        acc = None
        r0 = 0
        for y_ref in (ya_ref, yb_ref, yc_ref):
            r1 = r0 + y_ref.shape[-1]
            part = jnp.dot(y_ref[0, rows, :], w_ref[0, r0:r1, :], preferred_element_type=_F32)
            acc = part if acc is None else acc + part
            r0 = r1
        return acc

    acc_next = project(chunks[0])
    for i, rows in enumerate(chunks):
        acc = acc_next
        if i + 1 < len(chunks):
            acc_next = project(chunks[i + 1])
        xn = x_ref[0, rows, :] + mod[:, 2 * d:3 * d] * acc
        xo_ref[0, rows, :] = xn
        h_ref[0, rows, :] = _norm_mod(xn, g_ref[0], mod[:, 3 * d:4 * d],
                                      mod[:, 4 * d:5 * d]).astype(_BF16)


def _outproj(ya, yb, yc, x, mod, layer, mod_row, gain, w_out, tm):
    bsz, length, d = x.shape
    tm = min(tm, length)
    row = lambda b, i: (b, i, 0)
    return pl.pallas_call(
        _outproj_kernel,
        grid=(bsz, length // tm),
        in_specs=[pl.BlockSpec((1, tm, y.shape[-1]), row) for y in (ya, yb, yc)] + [
            pl.BlockSpec((1, tm, d), row),
            _mod_spec(mod, layer, mod_row, 2),
            pl.BlockSpec((1, 1, d), lambda b, i: (layer, 0, 0)),
            pl.BlockSpec((1,) + w_out.shape[1:], lambda b, i: (layer, 0, 0), pipeline_mode=pl.Buffered(1)),
        ],
        out_specs=[pl.BlockSpec((1, tm, d), row), pl.BlockSpec((1, tm, d), row)],
        out_shape=[jax.ShapeDtypeStruct((bsz, length, d), _F32),
                   jax.ShapeDtypeStruct((bsz, length, d), _BF16)],
        compiler_params=_cparams("parallel", "parallel"),
        name="out_proj_residual",
    )(ya, yb, yc, x, mod, gain, w_out)


def _mlp_kernel(h_ref, x_ref, mod_ref, wu_ref, wd_ref, o_ref):
    k = pl.program_id(2)
    d = x_ref.shape[-1]
    @pl.when(k == 0)
    def _():
        o_ref[...] = jnp.zeros_like(o_ref)

    u = jnp.dot(h_ref[0], wu_ref[0], preferred_element_type=_F32)
    act = jnp.square(jnp.maximum(u, 0.0)).astype(_BF16)
    o_ref[0] += jnp.dot(act, wd_ref[0], preferred_element_type=_F32)

    @pl.when(k == pl.num_programs(2) - 1)
    def _():
        o_ref[0] = x_ref[0] + mod_ref[0, 0, :, 5 * d:6 * d] * o_ref[0]


def _mlp(h, x, mod, layer, mod_row, w_up, w_down, tm, tf):
    bsz, length, d = x.shape
    d_ff = w_up.shape[-1]
    tm = min(tm, length)
    tf = min(tf, d_ff)
    row = lambda b, i, k: (b, i, 0)
    return pl.pallas_call(
        _mlp_kernel,
        grid=(bsz, length // tm, d_ff // tf),
        in_specs=[pl.BlockSpec((1, tm, d), row),
                  pl.BlockSpec((1, tm, d), row),
                  _mod_spec(mod, layer, mod_row, 3),
                  pl.BlockSpec((1, d, tf), lambda b, i, k: (layer, 0, k)),
                  pl.BlockSpec((1, tf, d), lambda b, i, k: (layer, k, 0))],
        out_specs=pl.BlockSpec((1, tm, d), row),
        out_shape=jax.ShapeDtypeStruct((bsz, length, d), _F32),
        compiler_params=_cparams("parallel", "parallel", "arbitrary"),
        name="mlp_sq_relu",
    )(h, x, mod, w_up, w_down)


def _logit_bounds(q_norm, k_norm, rel_pos_bias, sink_logits):
    gq = jnp.max(jnp.abs(q_norm), axis=-1)
    gk = jnp.max(jnp.abs(k_norm), axis=-1)
    qk_bound = HEAD_DIM * LOGIT2_SCALE * BOUND_MARGIN * gq * gk + BOUND_MARGIN - 1.0
    bias_max = jnp.maximum(jnp.max(rel_pos_bias, axis=(1, 2, 3)), 0.0) * LOG2_E
    sink_max = jnp.max(sink_logits, axis=-1) * LOG2_E
    return jnp.stack([qk_bound[:, 0], qk_bound[:, 1] + bias_max,
                      jnp.maximum(qk_bound[:, 2], sink_max)], axis=-1).astype(_F32)


def _rope_tables(length):
    t = np.arange(length)
    nf = HEAD_DIM // 4
    inv = ROPE_THETA ** (-jnp.arange(nf, dtype=_F32) / nf)
    ar = jnp.asarray(t // GRID_W, _F32)[:, None] * inv
    ac = jnp.asarray(t % GRID_W, _F32)[:, None] * inv
    ang = jnp.concatenate([ar, ar, ac, ac], axis=-1)
    cos, sin = jnp.cos(ang), jnp.sin(ang)
    first = (np.arange(HEAD_DIM) // nf) % 2 == 0
    return cos, jnp.where(first, -sin, 0.0), jnp.where(first, 0.0, sin)


def kernel(x, c, ctx, c_ctx, w_mod, b_mod, norm_attn, norm_mlp, w_in, q_norm, k_norm,
           rel_pos_bias, sink_logits, w_out, w_up, w_down):
    bsz, length, d = x.shape
    depth = w_mod.shape[0]
    assert length % (NB_QROWS * GRID_W) == 0 and length // GRID_W >= NB_KROWS
    assert w_in.shape[-1] == N_HEADS_IN * HEAD_DIM and w_out.shape[1] == N_HEADS_OUT * HEAD_DIM

    mod_rows = 8
    c_rows = jnp.zeros((mod_rows, d), _F32).at[:bsz].set(c).at[bsz].set(c_ctx)
    mod = _modulation(c_rows, w_mod, b_mod).reshape(depth, mod_rows, 1, N_MOD * d)

    w_in_b, w_out_b = w_in.astype(_BF16), w_out.astype(_BF16)
    w_up_b, w_down_b = w_up.astype(_BF16), w_down.astype(_BF16)
    g_attn = norm_attn.reshape(depth, 1, d)
    g_mlp = norm_mlp.reshape(depth, 1, d)
    rope_tabs = _rope_tables(length)
    bounds = _logit_bounds(q_norm, k_norm, rel_pos_bias, sink_logits)
    bias_tabs = _nb_bias_tables(rel_pos_bias, length // GRID_W, bounds[:, 1])
    sink_tabs = jnp.broadcast_to(
        sink_logits.reshape(depth, KV_C, G_C, 1), (depth, KV_C, G_C, HEAD_DIM)).astype(_F32)

    cs = ctx.reshape(1, bsz * ctx.shape[1], d)
    for l in range(depth):
        last = l == depth - 1
        qkv, vt = _qkv_proj(x, mod, l, None, g_attn, w_in_b, q_norm, k_norm, rope_tabs, tm=256)
        qkv_c, vt_c = _qkv_proj(cs, mod, l, bsz, g_attn, w_in_b, q_norm, k_norm, None, tm=512)
        ya = _attn_a(qkv, vt, qkv_c, vt_c, bounds, l, tq=1024, tk=1024)
        yb = _attn_b(qkv, qkv_c, bias_tabs, bounds, l, tiles=8)
        yc = _attn_c(qkv, qkv_c, sink_tabs, bounds, l, tq=4096)
        x, h = _outproj(ya, yb, yc, x, mod, l, None, g_mlp, w_out_b, tm=512)
        x = _mlp(h, x, mod, l, None, w_up_b, w_down_b, tm=512, tf=1024)
        if not last:
            ya_c, yb_c, yc_c = _attn_ctx(qkv_c, sink_tabs, l, bsz)
            cs, hc = _outproj(ya_c, yb_c, yc_c, cs, mod, l, bsz, g_mlp, w_out_b, tm=512)
            cs = _mlp(hc, cs, mod, l, bsz, w_up_b, w_down_b, tm=512, tf=1024)
    return x
```

```python
import functools

import numpy as np
import jax
import jax.numpy as jnp
from jax import lax
from jax.experimental import pallas as pl
from jax.experimental.pallas import tpu as pltpu

HEAD_DIM = 128
H_A, KV_A = 4, 2
H_B = 6
H_C, KV_C = 6, 2
G_A = H_A // KV_A
G_C = H_C // KV_C
GRID_W = 64
NA_ROWS = 8
NA_COLS = 16
WINDOW = 128
ROPE_THETA = 10000.0
EPS = 1e-6
N_MOD = 6
ATTN_SCALE = HEAD_DIM ** -0.5
NEG_INF = -1e30
LOG2_E = 1.4426950408889634
LOGIT2_SCALE = ATTN_SCALE * LOG2_E
BOUND_MARGIN = 1.02
MIN_ROW_SUM = 2.0 ** -64

_IN_QA, _IN_KA, _IN_VA = 0, H_A, H_A + KV_A
_IN_QB = _IN_VA + KV_A
_IN_KB, _IN_VB = _IN_QB + H_B, _IN_QB + 2 * H_B
_IN_QC = _IN_VB + H_B
_IN_KC, _IN_VC = _IN_QC + H_C, _IN_QC + H_C + KV_C
N_HEADS_IN = _IN_VC + KV_C
N_HEADS_OUT = H_A + H_B + H_C

QC0 = 0
QA0 = QC0 + H_C
KA0 = QA0 + H_A
VA0 = KA0 + KV_A
QB0 = VA0 + KV_A
KB0 = QB0 + H_B
VB0 = KB0 + H_B
KC0 = VB0 + H_B
VC0 = KC0 + KV_C
assert QA0 % G_A == 0 and QC0 % G_C == 0 and VC0 + KV_C == N_HEADS_IN

_REGIONS = (
    (_IN_QA, QA0, H_A, "q", 0, True), (_IN_KA, KA0, KV_A, "k", 0, True), (_IN_VA, VA0, KV_A, None, 0, False),
    (_IN_QB, QB0, H_B, "q", 1, False), (_IN_KB, KB0, H_B, "k", 1, False), (_IN_VB, VB0, H_B, None, 0, False),
    (_IN_QC, QC0, H_C, "q", 2, True), (_IN_KC, KC0, KV_C, "k", 2, True), (_IN_VC, VC0, KV_C, None, 0, False),
)

VT_ROWS = HEAD_DIM + 16
NB_QROWS = 4
NB_KROWS = NB_QROWS + NA_ROWS - 1
NB_HEADS_PER_STEP = 2
OUTPROJ_ROW_CHUNK = 256
VMEM_LIMIT_BYTES = 56 * 1024 * 1024

_BF16 = jnp.bfloat16
_F32 = jnp.float32
_NT = (((1,), (1,)), ((), ()))
_TN = (((0,), (0,)), ((), ()))


def _cparams(*sem):
    return pltpu.CompilerParams(dimension_semantics=sem, vmem_limit_bytes=VMEM_LIMIT_BYTES)


def _norm_mod(x, gain, shift, scale):
    y = x * lax.rsqrt(jnp.mean(x * x, axis=-1, keepdims=True) + EPS)
    return (y * gain) * (1.0 + scale) + shift


def _logits2_t(k, q):
    return lax.dot_general(k, q, _NT, preferred_element_type=_F32)


def _attend_logits(q, kvs):
    zs = []
    for k, _, bias2, keep in kvs:
        z = _logits2_t(k, q)
        if bias2 is not None:
            z = z + bias2
        if keep is not None:
            z = jnp.where(keep, z, NEG_INF)
        zs.append(z)
    return zs


def _attend_t(q, kvs, sink2_row=None, zs=None):
    if zs is None:
        zs = _attend_logits(q, kvs)
    m = zs[0].max(axis=0, keepdims=True)
    for z in zs[1:]:
        m = jnp.maximum(m, z.max(axis=0, keepdims=True))
    if sink2_row is not None:
        m = jnp.maximum(m, sink2_row)
    l = None
    acc = None
    for z, (_, v, _, _) in zip(zs, kvs):
        p = jnp.exp2(z - m)
        ls = p.sum(axis=0, keepdims=True)
        pv = lax.dot_general(v, p.astype(_BF16), _TN, preferred_element_type=_F32)
        l = ls if l is None else l + ls
        acc = pv if acc is None else acc + pv
    if sink2_row is not None:
        l = l + jnp.exp2(sink2_row - m)
    return acc / l


def _attend_units(units):
    outs = []
    zs_next = _attend_logits(*units[0][:2])
    for u, (q, kvs, sink2) in enumerate(units):
        zs = zs_next
        if u + 1 < len(units):
            zs_next = _attend_logits(*units[u + 1][:2])
        outs.append(_attend_t(q, kvs, sink2, zs))
    return outs


def _attend_units_shifted(units, bound):
    outs = []
    min_sum = None
    zs_next = _attend_logits(*units[0][:2])
    for u, (q, kvs, sink2) in enumerate(units):
        zs = zs_next
        if u + 1 < len(units):
            zs_next = _attend_logits(*units[u + 1][:2])
        l = None
        acc = None
        for z, (_, v, _, _) in zip(zs, kvs):
            p = jnp.exp2(z if bound is None else z - bound)
            ls = p.sum(axis=0, keepdims=True)
            pv = lax.dot_general(v, p.astype(_BF16), _TN, preferred_element_type=_F32)
            l = ls if l is None else l + ls
            acc = pv if acc is None else acc + pv
        if sink2 is not None:
            l = l + jnp.exp2(sink2 if bound is None else sink2 - bound)
        outs.append(acc / l)
        min_sum = l if min_sum is None else jnp.minimum(min_sum, l)
    return outs, jnp.min(min_sum)


def _mod_kernel(c_ref, w_ref, b_ref, o_ref):
    c = c_ref[...]
    cond = c * jax.nn.sigmoid(c)
    o_ref[0] = jnp.dot(cond.astype(_BF16), w_ref[0].astype(_BF16),
                       preferred_element_type=_F32) + b_ref[0]


def _modulation(c_rows, w_mod, b_mod):
    depth, d, n = w_mod.shape
    tn = next(t for t in (1024, 512, 256, 128, n) if n % t == 0)
    rows = c_rows.shape[0]
    return pl.pallas_call(
        _mod_kernel,
        grid=(depth, n // tn),
        in_specs=[pl.BlockSpec((rows, d), lambda l, j: (0, 0)),
                  pl.BlockSpec((1, d, tn), lambda l, j: (l, 0, j)),
                  pl.BlockSpec((1, 1, tn), lambda l, j: (l, 0, j))],
        out_specs=pl.BlockSpec((1, rows, tn), lambda l, j: (l, 0, j)),
        out_shape=jax.ShapeDtypeStruct((depth, rows, n), _F32),
        compiler_params=_cparams("parallel", "parallel"),
        name="adaln_modulation",
    )(c_rows, w_mod, b_mod.reshape(depth, 1, n))


def _mod_spec(mod, layer, mod_row, n_grid):
    block = (1, 1, 1, mod.shape[-1])
    if n_grid == 2:
        return pl.BlockSpec(block, (lambda b, i: (layer, b, 0, 0)) if mod_row is None
                            else (lambda b, i: (layer, mod_row, 0, 0)))
    return pl.BlockSpec(block, (lambda b, i, k: (layer, b, 0, 0)) if mod_row is None
                        else (lambda b, i, k: (layer, mod_row, 0, 0)))


def _qkv_kernel(x_ref, mod_ref, g_ref, w_ref, qn_ref, kn_ref, *rest, rope):
    if rope:
        cos_ref, sa_ref, sb_ref, o_ref, vt_ref = rest
    else:
        o_ref, vt_ref = rest
    d = x_ref.shape[-1]
    mod = mod_ref[0, 0]
    h = _norm_mod(x_ref[0], g_ref[0], mod[:, 0:d], mod[:, d:2 * d]).astype(_BF16)
    for c0, o0, nh, kind, row, rotary in _REGIONS:
        y = jnp.dot(h, w_ref[0, :, c0 * HEAD_DIM:(c0 + nh) * HEAD_DIM], preferred_element_type=_F32)
        for j in range(nh):
            yj = y[:, j * HEAD_DIM:(j + 1) * HEAD_DIM]
            if kind is not None:
                gain = (qn_ref if kind == "q" else kn_ref)[0, row:row + 1, :]
                yj = yj * lax.rsqrt(jnp.mean(yj * yj, axis=-1, keepdims=True) + EPS) * gain
                if rope and rotary:
                    yj = (yj * cos_ref[...]
                          + pltpu.roll(yj, HEAD_DIM - HEAD_DIM // 4, 1) * sa_ref[...]
                          + pltpu.roll(yj, HEAD_DIM // 4, 1) * sb_ref[...])
                if kind == "q":
                    yj = yj * LOGIT2_SCALE
            o_ref[0, o0 + j] = yj.astype(_BF16)
            if o0 == VA0:
                vt_ref[0, j, :HEAD_DIM, :] = yj.T.astype(_BF16)
                vt_ref[0, j, HEAD_DIM:, :] = jnp.ones((VT_ROWS - HEAD_DIM, yj.shape[0]), _BF16)


def _qkv_proj(x, mod, layer, mod_row, gain, w_in, qn, kn, rope_tabs, tm):
    bsz, length, d = x.shape
    d_in = w_in.shape[-1]
    tm = min(tm, length)
    in_specs = [
        pl.BlockSpec((1, tm, d), lambda b, i: (b, i, 0)),
        _mod_spec(mod, layer, mod_row, 2),
        pl.BlockSpec((1, 1, d), lambda b, i: (layer, 0, 0)),
        pl.BlockSpec((1, d, d_in), lambda b, i: (layer, 0, 0), pipeline_mode=pl.Buffered(1)),
        pl.BlockSpec((1,) + qn.shape[1:], lambda b, i: (layer, 0, 0)),
        pl.BlockSpec((1,) + kn.shape[1:], lambda b, i: (layer, 0, 0)),
    ]
    args = [x, mod, gain, w_in, qn, kn]
    if rope_tabs is not None:
        in_specs += [pl.BlockSpec((tm, HEAD_DIM), lambda b, i: (i, 0))] * 3
        args += list(rope_tabs)
    return pl.pallas_call(
        functools.partial(_qkv_kernel, rope=rope_tabs is not None),
        grid=(bsz, length // tm),
        in_specs=in_specs,
        out_specs=[pl.BlockSpec((1, N_HEADS_IN, tm, HEAD_DIM), lambda b, i: (b, 0, i, 0)),
                   pl.BlockSpec((1, KV_A, VT_ROWS, tm), lambda b, i: (b, 0, 0, i))],
        out_shape=[jax.ShapeDtypeStruct((bsz, N_HEADS_IN, length, HEAD_DIM), _BF16),
                   jax.ShapeDtypeStruct((bsz, KV_A, VT_ROWS, length), _BF16)],
        compiler_params=_cparams("parallel", "parallel"),
        name="qkv_proj_rope" if rope_tabs is not None else "qkv_proj_ctx",
    )(*args)


def _head_spec(rows, head0, heads=1):
    return pl.BlockSpec((1, heads, rows, HEAD_DIM), lambda b, h, t: (b, head0 // heads + h, 0, 0))


def _ctx_head_spec(rows, head0, heads=1):
    return pl.BlockSpec((1, heads, rows, HEAD_DIM), lambda b, h, t: (0, head0 // heads + h, b, 0))


def _query_spec(rows, head0, heads):
    return pl.BlockSpec((1, heads, rows, HEAD_DIM), lambda b, h, t: (b, head0 // heads + h, t, 0))


def _attn_a_kernel(bound_ref, q_ref, k_ref, vt_ref, kc_ref, vtc_ref, o_ref, acc_sc, s0_sc, s1_sc,
                   *, tk, layer):
    tq = q_ref.shape[2]
    length = k_ref.shape[2]
    nq = G_A * tq
    n_blocks = length // tk
    q2 = q_ref[0].reshape(nq, HEAD_DIM)

    def qk(kb):
        return _logits2_t(kb, q2)

    def keys(ref, j):
        return ref[0, 0, pl.ds(pl.multiple_of(j * tk, tk), tk), :]

    def values_t(j):
        return vt_ref[0, 0, :, pl.ds(pl.multiple_of(j * tk, tk), tk)]

    bound = bound_ref[layer, 0]

    def shifted_pv(kb, vb):
        p = jnp.exp2(qk(kb) - bound)
        pv = jnp.dot(vb[:HEAD_DIM], p.astype(_BF16), preferred_element_type=_F32)
        return pv, p.sum(axis=0, keepdims=True)

    pv, row_sum = shifted_pv(kc_ref[0, 0], vtc_ref[0, 0])
    acc_sc[:HEAD_DIM, :] = pv

    def shifted_block(j, row_sum):
        pv, block_sum = shifted_pv(keys(k_ref, j), values_t(j))
        acc_sc[:HEAD_DIM, :] += pv
        return row_sum + block_sum

    row_sum = lax.fori_loop(0, n_blocks, shifted_block, row_sum, unroll=min(4, n_blocks))
    acc_sc[HEAD_DIM:HEAD_DIM + 1, :] = row_sum

    @pl.when(jnp.min(row_sum) < MIN_ROW_SUM)
    def _():
        _attn_a_running_max(qk, keys, values_t, kc_ref, vtc_ref, k_ref, acc_sc, s0_sc, s1_sc, n_blocks, nq)

    o = (acc_sc[:HEAD_DIM, :] / acc_sc[HEAD_DIM:HEAD_DIM + 1, :]).T
    o_ref[0] = jnp.concatenate([o[g * tq:(g + 1) * tq] for g in range(G_A)], axis=1).astype(_BF16)


def _attn_a_running_max(qk, keys, values_t, kc_ref, vtc_ref, k_ref, acc_sc, s0_sc, s1_sc, n_blocks, nq):
    def softmax_pv(s, vb, m_prev):
        m_new = jnp.maximum(m_prev, s.max(axis=0, keepdims=True))
        alpha = jnp.exp2(m_prev - m_new)
        p = jnp.exp2(s - m_new)
        pv = jnp.dot(vb, p.astype(_BF16), preferred_element_type=_F32)
        acc_sc[...] = alpha * acc_sc[...] + pv
        return m_new

    acc_sc[...] = jnp.zeros_like(acc_sc)
    m0 = jnp.full((1, nq), NEG_INF, _F32)
    s0_sc[...] = qk(keys(k_ref, 0))
    m = softmax_pv(qk(kc_ref[0, 0]), vtc_ref[0, 0], m0)

    def pair(j0, m, last):
        s1_sc[...] = qk(keys(k_ref, j0 + 1))
        m = softmax_pv(s0_sc[...], values_t(j0), m)
        if not last:
            s0_sc[...] = qk(keys(k_ref, j0 + 2))
        return softmax_pv(s1_sc[...], values_t(j0 + 1), m)

    m = lax.fori_loop(0, n_blocks // 2 - 1, lambda jj, m: pair(2 * jj, m, False), m)
    pair(n_blocks - 2, m, True)


def _attn_a(qkv, vt, qkv_ctx, vt_ctx, bounds, layer, tq, tk):
    bsz, _, length, _ = qkv.shape
    n_ctx = qkv_ctx.shape[2] // bsz
    tq = min(tq, length)
    tk = min(tk, length // 2)
    assert length % (2 * tk) == 0
    return pl.pallas_call(
        functools.partial(_attn_a_kernel, tk=tk, layer=layer),
        grid=(bsz, KV_A, length // tq),
        in_specs=[pl.BlockSpec(memory_space=pltpu.SMEM),
                  _query_spec(tq, QA0, G_A), _head_spec(length, KA0),
                  pl.BlockSpec((1, 1, VT_ROWS, length), lambda b, h, i: (b, h, 0, 0)),
                  _ctx_head_spec(n_ctx, KA0),
                  pl.BlockSpec((1, 1, VT_ROWS, n_ctx), lambda b, h, i: (0, h, 0, b))],
        out_specs=pl.BlockSpec((1, tq, G_A * HEAD_DIM), lambda b, h, i: (b, i, h)),
        out_shape=jax.ShapeDtypeStruct((bsz, length, H_A * HEAD_DIM), _BF16),
        scratch_shapes=[pltpu.VMEM((VT_ROWS, G_A * tq), _F32),
                        pltpu.VMEM((tk, G_A * tq), _F32), pltpu.VMEM((tk, G_A * tq), _F32)],
        compiler_params=_cparams("parallel", "parallel", "parallel"),
        name="attn_global",
    )(bounds, qkv, qkv, vt, qkv_ctx, vt_ctx)


def _attn_b_kernel(bound_ref, q_ref, k_ref, v_ref, kc_ref, vc_ref, bias_ref, o_ref, *, layer):
    n_rows = k_ref.shape[2] // GRID_W
    tq = NB_QROWS * GRID_W
    nk = NB_KROWS * GRID_W
    tiles = q_ref.shape[2] // tq
    n_tiles = pl.num_programs(2) * tiles
    bound = bound_ref[layer, 1]
    units = []
    for i in range(tiles):
        t = pl.program_id(2) * tiles + i
        ks = jnp.clip(t * NB_QROWS - NA_ROWS // 2, 0, n_rows - NB_KROWS)
        off = pl.multiple_of(ks * GRID_W, GRID_W)
        variant = jnp.where(t == 0, 0, jnp.where(t == n_tiles - 1, 2, 1))
        for hh in range(NB_HEADS_PER_STEP):
            kvs = [(k_ref[0, hh, pl.ds(off, nk), :], v_ref[0, hh, pl.ds(off, nk), :],
                    bias_ref[0, hh, variant], None),
                   (kc_ref[0, hh], vc_ref[0, hh], -bound, None)]
            units.append((q_ref[0, hh, i * tq:(i + 1) * tq, :], kvs, None))

    def write(outs):
        for i in range(tiles):
            heads = outs[i * NB_HEADS_PER_STEP:(i + 1) * NB_HEADS_PER_STEP]
            o_ref[0, i * tq:(i + 1) * tq, :] = jnp.concatenate([o.T for o in heads], axis=1).astype(_BF16)

    outs, min_sum = _attend_units_shifted(units, None)
    write(outs)

    @pl.when(min_sum < MIN_ROW_SUM)
    def _():
        write(_attend_units(units))


def _nb_bias_tables(rpb, n_rows, shift):
    kh = NA_ROWS
    c = np.arange(GRID_W)[:, None, None]
    j = np.arange(2 * NA_COLS - 1)[None, :, None]
    ck = np.arange(GRID_W)[None, None, :]
    col_onehot = (j == ck - c + (NA_COLS - 1)).astype(np.float32)
    cs = np.clip(c - NA_COLS // 2, 0, GRID_W - NA_COLS)
    col_ok = ((ck >= cs) & (ck < cs + NA_COLS))[:, 0, :]
    blocks = jnp.einsum("lhrj,cjk->lhrkc", rpb * LOG2_E, col_onehot, precision=lax.Precision.HIGHEST)
    blocks = blocks - shift[:, None, None, None, None]
    blocks = jnp.where(col_ok.T, blocks, NEG_INF).astype(_F32)
    masked = jnp.full(blocks.shape[:2] + (GRID_W, GRID_W), NEG_INF, _F32)
    tabs = []
    for r0, ks in ((0, 0), (NB_QROWS, 0), (n_rows - NB_QROWS, n_rows - NB_KROWS)):
        key_rows = []
        for i in range(NB_KROWS):
            row = []
            for a in range(NB_QROWS):
                rq, rk = r0 + a, ks + i
                rs = min(max(rq - kh // 2, 0), n_rows - kh)
                row.append(blocks[:, :, rk - rq + (NA_ROWS - 1)] if rs <= rk < rs + kh else masked)
            key_rows.append(jnp.concatenate(row, axis=-1))
        tabs.append(jnp.concatenate(key_rows, axis=-2))
    return jnp.stack(tabs, axis=2)


def _attn_b(qkv, qkv_ctx, bias_tabs, bounds, layer, tiles):
    bsz, _, length, _ = qkv.shape
    n_ctx = qkv_ctx.shape[2] // bsz
    tq = min(tiles * NB_QROWS * GRID_W, length)
    assert length % tq == 0
    hps = NB_HEADS_PER_STEP
    return pl.pallas_call(
        functools.partial(_attn_b_kernel, layer=layer),
        grid=(bsz, H_B // hps, length // tq),
        in_specs=[pl.BlockSpec(memory_space=pltpu.SMEM),
                  _query_spec(tq, QB0, hps), _head_spec(length, KB0, hps), _head_spec(length, VB0, hps),
                  _ctx_head_spec(n_ctx, KB0, hps), _ctx_head_spec(n_ctx, VB0, hps),
                  pl.BlockSpec((1, hps) + bias_tabs.shape[2:], lambda b, h, t: (layer, h, 0, 0, 0))],
        out_specs=pl.BlockSpec((1, tq, hps * HEAD_DIM), lambda b, h, t: (b, t, h)),
        out_shape=jax.ShapeDtypeStruct((bsz, length, H_B * HEAD_DIM), _BF16),
        compiler_params=_cparams("parallel", "parallel", "parallel"),
        name="attn_neighbourhood",
    )(bounds, qkv, qkv, qkv, qkv_ctx, qkv_ctx, bias_tabs)


def _sink2_row(sink):
    return jnp.concatenate([sink[g:g + 1, :] for g in range(G_C)], axis=1) * LOG2_E


def _attn_c_kernel(bound_ref, q_ref, k_ref, v_ref, kc_ref, vc_ref, sink_ref, o_ref, *, layer):
    t = pl.program_id(2)
    tq = q_ref.shape[2]
    length = k_ref.shape[2]
    band = 3 * WINDOW
    nq = G_C * WINDOW
    kc = kc_ref[0, 0]
    vc = vc_ref[0, 0]
    sink2 = _sink2_row(sink_ref[0, 0])
    rel = (lax.broadcasted_iota(jnp.int32, (band, nq), 1) % WINDOW
           - lax.broadcasted_iota(jnp.int32, (band, nq), 0))
    units = []
    n_sub = tq // WINDOW
    keep_centred = jnp.abs(rel + WINDOW) <= WINDOW
    for n in range(n_sub):
        i0 = t * tq + n * WINDOW
        start = pl.multiple_of(jnp.clip(i0 - WINDOW, 0, length - band), WINDOW)
        kb = k_ref[0, 0, pl.ds(start, band), :]
        vb = v_ref[0, 0, pl.ds(start, band), :]
        q3 = q_ref[0, :, n * WINDOW:(n + 1) * WINDOW, :].reshape(nq, HEAD_DIM)
        keep = keep_centred if 0 < n < n_sub - 1 else jnp.abs(rel + (i0 - start)) <= WINDOW
        units.append((q3, [(kb, vb, None, keep), (kc, vc, None, None)], sink2))

    def write(outs):
        for n, o_t in enumerate(outs):
            o = o_t.T
            o_ref[0, n * WINDOW:(n + 1) * WINDOW, :] = jnp.concatenate(
                [o[g * WINDOW:(g + 1) * WINDOW] for g in range(G_C)], axis=1).astype(_BF16)

    outs, min_sum = _attend_units_shifted(units, bound_ref[layer, 2])
    write(outs)

    @pl.when(min_sum < MIN_ROW_SUM)
    def _():
        write(_attend_units(units))


def _attn_c(qkv, qkv_ctx, sink_tabs, bounds, layer, tq):
    bsz, _, length, _ = qkv.shape
    n_ctx = qkv_ctx.shape[2] // bsz
    tq = min(tq, length)
    return pl.pallas_call(
        functools.partial(_attn_c_kernel, layer=layer),
        grid=(bsz, KV_C, length // tq),
        in_specs=[pl.BlockSpec(memory_space=pltpu.SMEM),
                  _query_spec(tq, QC0, G_C), _head_spec(length, KC0), _head_spec(length, VC0),
                  _ctx_head_spec(n_ctx, KC0), _ctx_head_spec(n_ctx, VC0),
                  pl.BlockSpec((1, 1, G_C, HEAD_DIM), lambda b, h, t: (layer, h, 0, 0))],
        out_specs=pl.BlockSpec((1, tq, G_C * HEAD_DIM), lambda b, h, t: (b, t, h)),
        out_shape=jax.ShapeDtypeStruct((bsz, length, H_C * HEAD_DIM), _BF16),
        compiler_params=_cparams("parallel", "parallel", "parallel"),
        name="attn_window",
    )(bounds, qkv, qkv, qkv, qkv_ctx, qkv_ctx, sink_tabs)


def _attn_ctx_kernel(qkv_ref, sink_ref, ya_ref, yb_ref, yc_ref):
    n = qkv_ref.shape[2]

    def head(u):
        return qkv_ref[0, u]

    def heads(u0, g):
        return qkv_ref[0, u0:u0 + g].reshape(g * n, HEAD_DIM)

    for kv in range(KV_A):
        o = _attend_t(heads(QA0 + kv * G_A, G_A), [(head(KA0 + kv), head(VA0 + kv), None, None)]).T
        for g in range(G_A):
            u = kv * G_A + g
            ya_ref[0, :, u * HEAD_DIM:(u + 1) * HEAD_DIM] = o[g * n:(g + 1) * n].astype(_BF16)
    for hb in range(H_B):
        o = _attend_t(head(QB0 + hb), [(head(KB0 + hb), head(VB0 + hb), None, None)]).T
        yb_ref[0, :, hb * HEAD_DIM:(hb + 1) * HEAD_DIM] = o.astype(_BF16)
    for kv in range(KV_C):
        sink = sink_ref[0, kv]
        sink2 = jnp.concatenate(
            [jnp.broadcast_to(sink[g:g + 1, :1], (1, n)) for g in range(G_C)], axis=1) * LOG2_E
        o = _attend_t(heads(QC0 + kv * G_C, G_C), [(head(KC0 + kv), head(VC0 + kv), None, None)], sink2).T
        for g in range(G_C):
            u = kv * G_C + g
            yc_ref[0, :, u * HEAD_DIM:(u + 1) * HEAD_DIM] = o[g * n:(g + 1) * n].astype(_BF16)


def _attn_ctx(qkv_ctx, sink_tabs, layer, bsz):
    n_heads, rows = qkv_ctx.shape[1:3]
    n = rows // bsz
    widths = (H_A * HEAD_DIM, H_B * HEAD_DIM, H_C * HEAD_DIM)
    return pl.pallas_call(
        _attn_ctx_kernel,
        grid=(bsz,),
        in_specs=[pl.BlockSpec((1, n_heads, n, HEAD_DIM), lambda b: (0, 0, b, 0)),
                  pl.BlockSpec((1,) + sink_tabs.shape[1:], lambda b: (layer, 0, 0, 0))],
        out_specs=[pl.BlockSpec((1, n, w), lambda b: (0, b, 0)) for w in widths],
        out_shape=[jax.ShapeDtypeStruct((1, rows, w), _BF16) for w in widths],
        compiler_params=_cparams("parallel"),
        name="attn_ctx",
    )(qkv_ctx, sink_tabs)


def _outproj_kernel(ya_ref, yb_ref, yc_ref, x_ref, mod_ref, g_ref, w_ref, xo_ref, h_ref):
    tm, d = x_ref.shape[1:]
    mod = mod_ref[0, 0]
    chunks = [slice(m0, min(m0 + OUTPROJ_ROW_CHUNK, tm)) for m0 in range(0, tm, OUTPROJ_ROW_CHUNK)]

    def project(rows):
        acc = None
        r0 = 0
        for y_ref in (ya_ref, yb_ref, yc_ref):
            r1 = r0 + y_ref.shape[-1]
            part = jnp.dot(y_ref[0, rows, :], w_ref[0, r0:r1, :], preferred_element_type=_F32)
            acc = part if acc is None else acc + part
            r0 = r1
        return acc

    acc_next = project(chunks[0])
    for i, rows in enumerate(chunks):
        acc = acc_next
        if i + 1 < len(chunks):
            acc_next = project(chunks[i + 1])
        xn = x_ref[0, rows, :] + mod[:, 2 * d:3 * d] * acc
        xo_ref[0, rows, :] = xn
        h_ref[0, rows, :] = _norm_mod(xn, g_ref[0], mod[:, 3 * d:4 * d],
                                      mod[:, 4 * d:5 * d]).astype(_BF16)


def _outproj(ya, yb, yc, x, mod, layer, mod_row, gain, w_out, tm):
    bsz, length, d = x.shape
    tm = min(tm, length)
    row = lambda b, i: (b, i, 0)
    return pl.pallas_call(
        _outproj_kernel,
        grid=(bsz, length // tm),
        in_specs=[pl.BlockSpec((1, tm, y.shape[-1]), row) for y in (ya, yb, yc)] + [
            pl.BlockSpec((1, tm, d), row),
            _mod_spec(mod, layer, mod_row, 2),
            pl.BlockSpec((1, 1, d), lambda b, i: (layer, 0, 0)),
            pl.BlockSpec((1,) + w_out.shape[1:], lambda b, i: (layer, 0, 0), pipeline_mode=pl.Buffered(1)),
        ],
        out_specs=[pl.BlockSpec((1, tm, d), row), pl.BlockSpec((1, tm, d), row)],
        out_shape=[jax.ShapeDtypeStruct((bsz, length, d), _F32),
                   jax.ShapeDtypeStruct((bsz, length, d), _BF16)],
        compiler_params=_cparams("parallel", "parallel"),
        name="out_proj_residual",
    )(ya, yb, yc, x, mod, gain, w_out)


def _mlp_kernel(h_ref, x_ref, mod_ref, wu_ref, wd_ref, o_ref):
    k = pl.program_id(2)
    d = x_ref.shape[-1]
    @pl.when(k == 0)
    def _():
        o_ref[...] = jnp.zeros_like(o_ref)

    u = jnp.dot(h_ref[0], wu_ref[0], preferred_element_type=_F32)
    act = jnp.square(jnp.maximum(u, 0.0)).astype(_BF16)
    o_ref[0] += jnp.dot(act, wd_ref[0], preferred_element_type=_F32)

    @pl.when(k == pl.num_programs(2) - 1)
    def _():
        o_ref[0] = x_ref[0] + mod_ref[0, 0, :, 5 * d:6 * d] * o_ref[0]


def _mlp(h, x, mod, layer, mod_row, w_up, w_down, tm, tf):
    bsz, length, d = x.shape
    d_ff = w_up.shape[-1]
    tm = min(tm, length)
    tf = min(tf, d_ff)
    row = lambda b, i, k: (b, i, 0)
    return pl.pallas_call(
        _mlp_kernel,
        grid=(bsz, length // tm, d_ff // tf),
        in_specs=[pl.BlockSpec((1, tm, d), row),
                  pl.BlockSpec((1, tm, d), row),
                  _mod_spec(mod, layer, mod_row, 3),
                  pl.BlockSpec((1, d, tf), lambda b, i, k: (layer, 0, k)),
                  pl.BlockSpec((1, tf, d), lambda b, i, k: (layer, k, 0))],
        out_specs=pl.BlockSpec((1, tm, d), row),
        out_shape=jax.ShapeDtypeStruct((bsz, length, d), _F32),
        compiler_params=_cparams("parallel", "parallel", "arbitrary"),
        name="mlp_sq_relu",
    )(h, x, mod, w_up, w_down)


def _logit_bounds(q_norm, k_norm, rel_pos_bias, sink_logits):
    gq = jnp.max(jnp.abs(q_norm), axis=-1)
    gk = jnp.max(jnp.abs(k_norm), axis=-1)
    qk_bound = HEAD_DIM * LOGIT2_SCALE * BOUND_MARGIN * gq * gk + BOUND_MARGIN - 1.0
    bias_max = jnp.maximum(jnp.max(rel_pos_bias, axis=(1, 2, 3)), 0.0) * LOG2_E
    sink_max = jnp.max(sink_logits, axis=-1) * LOG2_E
    return jnp.stack([qk_bound[:, 0], qk_bound[:, 1] + bias_max,
                      jnp.maximum(qk_bound[:, 2], sink_max)], axis=-1).astype(_F32)


def _rope_tables(length):
    t = np.arange(length)
    nf = HEAD_DIM // 4
    inv = ROPE_THETA ** (-jnp.arange(nf, dtype=_F32) / nf)
    ar = jnp.asarray(t // GRID_W, _F32)[:, None] * inv
    ac = jnp.asarray(t % GRID_W, _F32)[:, None] * inv
    ang = jnp.concatenate([ar, ar, ac, ac], axis=-1)
    cos, sin = jnp.cos(ang), jnp.sin(ang)
    first = (np.arange(HEAD_DIM) // nf) % 2 == 0
    return cos, jnp.where(first, -sin, 0.0), jnp.where(first, 0.0, sin)


def kernel(x, c, ctx, c_ctx, w_mod, b_mod, norm_attn, norm_mlp, w_in, q_norm, k_norm,
           rel_pos_bias, sink_logits, w_out, w_up, w_down):
    bsz, length, d = x.shape
    depth = w_mod.shape[0]
    assert length % (NB_QROWS * GRID_W) == 0 and length // GRID_W >= NB_KROWS
    assert w_in.shape[-1] == N_HEADS_IN * HEAD_DIM and w_out.shape[1] == N_HEADS_OUT * HEAD_DIM

    mod_rows = 8
    c_rows = jnp.zeros((mod_rows, d), _F32).at[:bsz].set(c).at[bsz].set(c_ctx)
    mod = _modulation(c_rows, w_mod, b_mod).reshape(depth, mod_rows, 1, N_MOD * d)

    w_in_b, w_out_b = w_in.astype(_BF16), w_out.astype(_BF16)
    w_up_b, w_down_b = w_up.astype(_BF16), w_down.astype(_BF16)
    g_attn = norm_attn.reshape(depth, 1, d)
    g_mlp = norm_mlp.reshape(depth, 1, d)
    rope_tabs = _rope_tables(length)
    bounds = _logit_bounds(q_norm, k_norm, rel_pos_bias, sink_logits)
    bias_tabs = _nb_bias_tables(rel_pos_bias, length // GRID_W, bounds[:, 1])
    sink_tabs = jnp.broadcast_to(
        sink_logits.reshape(depth, KV_C, G_C, 1), (depth, KV_C, G_C, HEAD_DIM)).astype(_F32)

    cs = ctx.reshape(1, bsz * ctx.shape[1], d)
    for l in range(depth):
        last = l == depth - 1
        qkv, vt = _qkv_proj(x, mod, l, None, g_attn, w_in_b, q_norm, k_norm, rope_tabs, tm=256)
        qkv_c, vt_c = _qkv_proj(cs, mod, l, bsz, g_attn, w_in_b, q_norm, k_norm, None, tm=512)
        ya = _attn_a(qkv, vt, qkv_c, vt_c, bounds, l, tq=1024, tk=1024)
        yb = _attn_b(qkv, qkv_c, bias_tabs, bounds, l, tiles=8)
        yc = _attn_c(qkv, qkv_c, sink_tabs, bounds, l, tq=4096)
        x, h = _outproj(ya, yb, yc, x, mod, l, None, g_mlp, w_out_b, tm=512)
        x = _mlp(h, x, mod, l, None, w_up_b, w_down_b, tm=512, tf=1024)
        if not last:
            ya_c, yb_c, yc_c = _attn_ctx(qkv_c, sink_tabs, l, bsz)
            cs, hc = _outproj(ya_c, yb_c, yc_c, cs, mod, l, bsz, g_mlp, w_out_b, tm=512)
            cs = _mlp(hc, cs, mod, l, bsz, w_up_b, w_down_b, tm=512, tf=1024)
    return x
```

```python
import functools

import numpy as np
import jax
import jax.numpy as jnp
from jax import lax
from jax.experimental import pallas as pl
from jax.experimental.pallas import tpu as pltpu

HEAD_DIM = 128
H_A, KV_A = 4, 2
H_B = 6
H_C, KV_C = 6, 2
G_A = H_A // KV_A
G_C = H_C // KV_C
GRID_W = 64
NA_ROWS = 8
NA_COLS = 16
WINDOW = 128
ROPE_THETA = 10000.0
EPS = 1e-6
N_MOD = 6
ATTN_SCALE = HEAD_DIM ** -0.5
NEG_INF = -1e30
LOG2_E = 1.4426950408889634
LOGIT2_SCALE = ATTN_SCALE * LOG2_E
BOUND_MARGIN = 1.02
MIN_ROW_SUM = 2.0 ** -64

_IN_QA, _IN_KA, _IN_VA = 0, H_A, H_A + KV_A
_IN_QB = _IN_VA + KV_A
_IN_KB, _IN_VB = _IN_QB + H_B, _IN_QB + 2 * H_B
_IN_QC = _IN_VB + H_B
_IN_KC, _IN_VC = _IN_QC + H_C, _IN_QC + H_C + KV_C
N_HEADS_IN = _IN_VC + KV_C
N_HEADS_OUT = H_A + H_B + H_C

QC0 = 0
QA0 = QC0 + H_C
KA0 = QA0 + H_A
VA0 = KA0 + KV_A
QB0 = VA0 + KV_A
KB0 = QB0 + H_B
VB0 = KB0 + H_B
KC0 = VB0 + H_B
VC0 = KC0 + KV_C
assert QA0 % G_A == 0 and QC0 % G_C == 0 and VC0 + KV_C == N_HEADS_IN

_REGIONS = (
    (_IN_QA, QA0, H_A, "q", 0, True), (_IN_KA, KA0, KV_A, "k", 0, True),
    (_IN_QC, QC0, H_C, "q", 2, True), (_IN_KC, KC0, KV_C, "k", 2, True),
    (_IN_QB, QB0, H_B, "q", 1, False), (_IN_KB, KB0, H_B, "k", 1, False),
    (_IN_VA, VA0, KV_A, None, 0, False), (_IN_VC, VC0, KV_C, None, 0, False), (_IN_VB, VB0, H_B, None, 0, False),
)

VT_ROWS = HEAD_DIM + 16
NB_QROWS = 4
NB_KROWS = NB_QROWS + NA_ROWS - 1
NB_HEADS_PER_STEP = 2
OUTPROJ_ROW_CHUNK = 256
VMEM_LIMIT_BYTES = 56 * 1024 * 1024

_BF16 = jnp.bfloat16
_F32 = jnp.float32
_NT = (((1,), (1,)), ((), ()))
_TN = (((0,), (0,)), ((), ()))


def _cparams(*sem):
    return pltpu.CompilerParams(dimension_semantics=sem, vmem_limit_bytes=VMEM_LIMIT_BYTES)


def _norm_mod(x, gain, shift, scale):
    y = x * lax.rsqrt(jnp.mean(x * x, axis=-1, keepdims=True) + EPS)
    return (y * gain) * (1.0 + scale) + shift


def _logits2_t(k, q):
    return lax.dot_general(k, q, _NT, preferred_element_type=_F32)


def _attend_logits(q, kvs):
    zs = []
    for k, _, bias2, keep in kvs:
        z = _logits2_t(k, q)
        if bias2 is not None:
            z = z + bias2
        if keep is not None:
            z = jnp.where(keep, z, NEG_INF)
        zs.append(z)
    return zs


def _attend_t(q, kvs, sink2_row=None, zs=None):
    if zs is None:
        zs = _attend_logits(q, kvs)
    m = zs[0].max(axis=0, keepdims=True)
    for z in zs[1:]:
        m = jnp.maximum(m, z.max(axis=0, keepdims=True))
    if sink2_row is not None:
        m = jnp.maximum(m, sink2_row)
    l = None
    acc = None
    for z, (_, v, _, _) in zip(zs, kvs):
        p = jnp.exp2(z - m)
        ls = p.sum(axis=0, keepdims=True)
        pv = lax.dot_general(v, p.astype(_BF16), _TN, preferred_element_type=_F32)
        l = ls if l is None else l + ls
        acc = pv if acc is None else acc + pv
    if sink2_row is not None:
        l = l + jnp.exp2(sink2_row - m)
    return acc / l


def _attend_units(units):
    outs = []
    zs_next = _attend_logits(*units[0][:2])
    for u, (q, kvs, sink2) in enumerate(units):
        zs = zs_next
        if u + 1 < len(units):
            zs_next = _attend_logits(*units[u + 1][:2])
        outs.append(_attend_t(q, kvs, sink2, zs))
    return outs


def _attend_units_shifted(units, bound):
    outs = []
    min_sum = None
    zs_next = _attend_logits(*units[0][:2])
    for u, (q, kvs, sink2) in enumerate(units):
        zs = zs_next
        if u + 1 < len(units):
            zs_next = _attend_logits(*units[u + 1][:2])
        l = None
        acc = None
        for z, (_, v, _, _) in zip(zs, kvs):
            p = jnp.exp2(z if bound is None else z - bound)
            ls = p.sum(axis=0, keepdims=True)
            pv = lax.dot_general(v, p.astype(_BF16), _TN, preferred_element_type=_F32)
            l = ls if l is None else l + ls
            acc = pv if acc is None else acc + pv
        if sink2 is not None:
            l = l + jnp.exp2(sink2 if bound is None else sink2 - bound)
        outs.append(acc / l)
        min_sum = l if min_sum is None else jnp.minimum(min_sum, l)
    return outs, jnp.min(min_sum)


def _mod_kernel(c_ref, w_ref, b_ref, o_ref):
    c = c_ref[...]
    cond = c * jax.nn.sigmoid(c)
    o_ref[0] = jnp.dot(cond.astype(_BF16), w_ref[0].astype(_BF16),
                       preferred_element_type=_F32) + b_ref[0]


def _modulation(c_rows, w_mod, b_mod):
    depth, d, n = w_mod.shape
    tn = next(t for t in (1024, 512, 256, 128, n) if n % t == 0)
    rows = c_rows.shape[0]
    return pl.pallas_call(
        _mod_kernel,
        grid=(depth, n // tn),
        in_specs=[pl.BlockSpec((rows, d), lambda l, j: (0, 0)),
                  pl.BlockSpec((1, d, tn), lambda l, j: (l, 0, j)),
                  pl.BlockSpec((1, 1, tn), lambda l, j: (l, 0, j))],
        out_specs=pl.BlockSpec((1, rows, tn), lambda l, j: (l, 0, j)),
        out_shape=jax.ShapeDtypeStruct((depth, rows, n), _F32),
        compiler_params=_cparams("parallel", "parallel"),
        name="adaln_modulation",
    )(c_rows, w_mod, b_mod.reshape(depth, 1, n))


def _mod_spec(mod, layer, mod_row, n_grid):
    block = (1, 1, 1, mod.shape[-1])
    if n_grid == 2:
        return pl.BlockSpec(block, (lambda b, i: (layer, b, 0, 0)) if mod_row is None
                            else (lambda b, i: (layer, mod_row, 0, 0)))
    return pl.BlockSpec(block, (lambda b, i, k: (layer, b, 0, 0)) if mod_row is None
                        else (lambda b, i, k: (layer, mod_row, 0, 0)))


def _qkv_kernel(x_ref, mod_ref, g_ref, w_ref, qn_ref, kn_ref, *rest, rope):
    if rope:
        cos_ref, sa_ref, sb_ref, o_ref, vt_ref = rest
    else:
        o_ref, vt_ref = rest
    d = x_ref.shape[-1]
    mod = mod_ref[0, 0]
    h = _norm_mod(x_ref[0], g_ref[0], mod[:, 0:d], mod[:, d:2 * d]).astype(_BF16)
    for c0, o0, nh, kind, row, rotary in _REGIONS:
        y = jnp.dot(h, w_ref[0, :, c0 * HEAD_DIM:(c0 + nh) * HEAD_DIM], preferred_element_type=_F32)
        for j in range(nh):
            yj = y[:, j * HEAD_DIM:(j + 1) * HEAD_DIM]
            if kind is not None:
                gain = (qn_ref if kind == "q" else kn_ref)[0, row:row + 1, :]
                yj = yj * lax.rsqrt(jnp.mean(yj * yj, axis=-1, keepdims=True) + EPS) * gain
                if rope and rotary:
                    yj = (yj * cos_ref[...]
                          + pltpu.roll(yj, HEAD_DIM - HEAD_DIM // 4, 1) * sa_ref[...]
                          + pltpu.roll(yj, HEAD_DIM // 4, 1) * sb_ref[...])
                if kind == "q":
                    yj = yj * LOGIT2_SCALE
            o_ref[0, o0 + j] = yj.astype(_BF16)
            if o0 == VA0:
                vt_ref[0, j, :HEAD_DIM, :] = yj.T.astype(_BF16)
                vt_ref[0, j, HEAD_DIM:, :] = jnp.ones((VT_ROWS - HEAD_DIM, yj.shape[0]), _BF16)


def _qkv_proj(x, mod, layer, mod_row, gain, w_in, qn, kn, rope_tabs, tm):
    bsz, length, d = x.shape
    d_in = w_in.shape[-1]
    tm = min(tm, length)
    in_specs = [
        pl.BlockSpec((1, tm, d), lambda b, i: (b, i, 0)),
        _mod_spec(mod, layer, mod_row, 2),
        pl.BlockSpec((1, 1, d), lambda b, i: (layer, 0, 0)),
        pl.BlockSpec((1, d, d_in), lambda b, i: (layer, 0, 0), pipeline_mode=pl.Buffered(1)),
        pl.BlockSpec((1,) + qn.shape[1:], lambda b, i: (layer, 0, 0)),
        pl.BlockSpec((1,) + kn.shape[1:], lambda b, i: (layer, 0, 0)),
    ]
    args = [x, mod, gain, w_in, qn, kn]
    if rope_tabs is not None:
        in_specs += [pl.BlockSpec((tm, HEAD_DIM), lambda b, i: (i, 0))] * 3
        args += list(rope_tabs)
    return pl.pallas_call(
        functools.partial(_qkv_kernel, rope=rope_tabs is not None),
        grid=(bsz, length // tm),
        in_specs=in_specs,
        out_specs=[pl.BlockSpec((1, N_HEADS_IN, tm, HEAD_DIM), lambda b, i: (b, 0, i, 0)),
                   pl.BlockSpec((1, KV_A, VT_ROWS, tm), lambda b, i: (b, 0, 0, i))],
        out_shape=[jax.ShapeDtypeStruct((bsz, N_HEADS_IN, length, HEAD_DIM), _BF16),
                   jax.ShapeDtypeStruct((bsz, KV_A, VT_ROWS, length), _BF16)],
        compiler_params=_cparams("parallel", "parallel"),
        name="qkv_proj_rope" if rope_tabs is not None else "qkv_proj_ctx",
    )(*args)


def _head_spec(rows, head0, heads=1):
    return pl.BlockSpec((1, heads, rows, HEAD_DIM), lambda b, h, t: (b, head0 // heads + h, 0, 0))


def _ctx_head_spec(rows, head0, heads=1):
    return pl.BlockSpec((1, heads, rows, HEAD_DIM), lambda b, h, t: (0, head0 // heads + h, b, 0))


def _query_spec(rows, head0, heads):
    return pl.BlockSpec((1, heads, rows, HEAD_DIM), lambda b, h, t: (b, head0 // heads + h, t, 0))


def _attn_a_kernel(bound_ref, q_ref, k_ref, vt_ref, kc_ref, vtc_ref, o_ref, acc_sc, s0_sc, s1_sc,
                   *, tk, layer):
    tq = q_ref.shape[2]
    length = k_ref.shape[2]
    nq = G_A * tq
    n_blocks = length // tk
    q2 = q_ref[0].reshape(nq, HEAD_DIM)

    def qk(kb):
        return _logits2_t(kb, q2)

    def keys(ref, j):
        return ref[0, 0, pl.ds(pl.multiple_of(j * tk, tk), tk), :]

    def values_t(j):
        return vt_ref[0, 0, :, pl.ds(pl.multiple_of(j * tk, tk), tk)]

    bound = bound_ref[layer, 0]

    def shifted_pv(kb, vb):
        p = jnp.exp2(qk(kb) - bound)
        pv = jnp.dot(vb[:HEAD_DIM], p.astype(_BF16), preferred_element_type=_F32)
        return pv, p.sum(axis=0, keepdims=True)

    pv, row_sum = shifted_pv(kc_ref[0, 0], vtc_ref[0, 0])
    acc_sc[:HEAD_DIM, :] = pv

    def shifted_block(j, row_sum):
        pv, block_sum = shifted_pv(keys(k_ref, j), values_t(j))
        acc_sc[:HEAD_DIM, :] += pv
        return row_sum + block_sum

    row_sum = lax.fori_loop(0, n_blocks, shifted_block, row_sum, unroll=min(4, n_blocks))
    acc_sc[HEAD_DIM:HEAD_DIM + 1, :] = row_sum

    @pl.when(jnp.min(row_sum) < MIN_ROW_SUM)
    def _():
        _attn_a_running_max(qk, keys, values_t, kc_ref, vtc_ref, k_ref, acc_sc, s0_sc, s1_sc, n_blocks, nq)

    o = (acc_sc[:HEAD_DIM, :] / acc_sc[HEAD_DIM:HEAD_DIM + 1, :]).T
    o_ref[0] = jnp.concatenate([o[g * tq:(g + 1) * tq] for g in range(G_A)], axis=1).astype(_BF16)


def _attn_a_running_max(qk, keys, values_t, kc_ref, vtc_ref, k_ref, acc_sc, s0_sc, s1_sc, n_blocks, nq):
    def softmax_pv(s, vb, m_prev):
        m_new = jnp.maximum(m_prev, s.max(axis=0, keepdims=True))
        alpha = jnp.exp2(m_prev - m_new)
        p = jnp.exp2(s - m_new)
        pv = jnp.dot(vb, p.astype(_BF16), preferred_element_type=_F32)
        acc_sc[...] = alpha * acc_sc[...] + pv
        return m_new

    acc_sc[...] = jnp.zeros_like(acc_sc)
    m0 = jnp.full((1, nq), NEG_INF, _F32)
    s0_sc[...] = qk(keys(k_ref, 0))
    m = softmax_pv(qk(kc_ref[0, 0]), vtc_ref[0, 0], m0)

    def pair(j0, m, last):
        s1_sc[...] = qk(keys(k_ref, j0 + 1))
        m = softmax_pv(s0_sc[...], values_t(j0), m)
        if not last:
            s0_sc[...] = qk(keys(k_ref, j0 + 2))
        return softmax_pv(s1_sc[...], values_t(j0 + 1), m)

    m = lax.fori_loop(0, n_blocks // 2 - 1, lambda jj, m: pair(2 * jj, m, False), m)
    pair(n_blocks - 2, m, True)


def _attn_a(qkv, vt, qkv_ctx, vt_ctx, bounds, layer, tq, tk):
    bsz, _, length, _ = qkv.shape
    n_ctx = qkv_ctx.shape[2] // bsz
    tq = min(tq, length)
    tk = min(tk, length // 2)
    assert length % (2 * tk) == 0
    return pl.pallas_call(
        functools.partial(_attn_a_kernel, tk=tk, layer=layer),
        grid=(bsz, KV_A, length // tq),
        in_specs=[pl.BlockSpec(memory_space=pltpu.SMEM),
                  _query_spec(tq, QA0, G_A), _head_spec(length, KA0),
                  pl.BlockSpec((1, 1, VT_ROWS, length), lambda b, h, i: (b, h, 0, 0)),
                  _ctx_head_spec(n_ctx, KA0),
                  pl.BlockSpec((1, 1, VT_ROWS, n_ctx), lambda b, h, i: (0, h, 0, b))],
        out_specs=pl.BlockSpec((1, tq, G_A * HEAD_DIM), lambda b, h, i: (b, i, h)),
        out_shape=jax.ShapeDtypeStruct((bsz, length, H_A * HEAD_DIM), _BF16),
        scratch_shapes=[pltpu.VMEM((VT_ROWS, G_A * tq), _F32),
                        pltpu.VMEM((tk, G_A * tq), _F32), pltpu.VMEM((tk, G_A * tq), _F32)],
        compiler_params=_cparams("parallel", "parallel", "parallel"),
        name="attn_global",
    )(bounds, qkv, qkv, vt, qkv_ctx, vt_ctx)


def _attn_b_kernel(bound_ref, q_ref, k_ref, v_ref, kc_ref, vc_ref, bias_ref, o_ref, *, layer):
    n_rows = k_ref.shape[2] // GRID_W
    tq = NB_QROWS * GRID_W
    nk = NB_KROWS * GRID_W
    tiles = q_ref.shape[2] // tq
    n_tiles = pl.num_programs(2) * tiles
    bound = bound_ref[layer, 1]
    units = []
    for i in range(tiles):
        t = pl.program_id(2) * tiles + i
        ks = jnp.clip(t * NB_QROWS - NA_ROWS // 2, 0, n_rows - NB_KROWS)
        off = pl.multiple_of(ks * GRID_W, GRID_W)
        variant = jnp.where(t == 0, 0, jnp.where(t == n_tiles - 1, 2, 1))
        for hh in range(NB_HEADS_PER_STEP):
            kvs = [(k_ref[0, hh, pl.ds(off, nk), :], v_ref[0, hh, pl.ds(off, nk), :],
                    bias_ref[0, hh, variant], None),
                   (kc_ref[0, hh], vc_ref[0, hh], -bound, None)]
            units.append((q_ref[0, hh, i * tq:(i + 1) * tq, :], kvs, None))

    def write(outs):
        for i in range(tiles):
            heads = outs[i * NB_HEADS_PER_STEP:(i + 1) * NB_HEADS_PER_STEP]
            o_ref[0, i * tq:(i + 1) * tq, :] = jnp.concatenate([o.T for o in heads], axis=1).astype(_BF16)

    outs, min_sum = _attend_units_shifted(units, None)
    write(outs)

    @pl.when(min_sum < MIN_ROW_SUM)
    def _():
        write(_attend_units(units))


def _nb_bias_tables(rpb, n_rows, shift):
    kh = NA_ROWS
    c = np.arange(GRID_W)[:, None, None]
    j = np.arange(2 * NA_COLS - 1)[None, :, None]
    ck = np.arange(GRID_W)[None, None, :]
    col_onehot = (j == ck - c + (NA_COLS - 1)).astype(np.float32)
    cs = np.clip(c - NA_COLS // 2, 0, GRID_W - NA_COLS)
    col_ok = ((ck >= cs) & (ck < cs + NA_COLS))[:, 0, :]
    blocks = jnp.einsum("lhrj,cjk->lhrkc", rpb * LOG2_E, col_onehot, precision=lax.Precision.HIGHEST)
    blocks = blocks - shift[:, None, None, None, None]
    blocks = jnp.where(col_ok.T, blocks, NEG_INF).astype(_F32)
    masked = jnp.full(blocks.shape[:2] + (GRID_W, GRID_W), NEG_INF, _F32)
    tabs = []
    for r0, ks in ((0, 0), (NB_QROWS, 0), (n_rows - NB_QROWS, n_rows - NB_KROWS)):
        key_rows = []
        for i in range(NB_KROWS):
            row = []
            for a in range(NB_QROWS):
                rq, rk = r0 + a, ks + i
                rs = min(max(rq - kh // 2, 0), n_rows - kh)
                row.append(blocks[:, :, rk - rq + (NA_ROWS - 1)] if rs <= rk < rs + kh else masked)
            key_rows.append(jnp.concatenate(row, axis=-1))
        tabs.append(jnp.concatenate(key_rows, axis=-2))
    return jnp.stack(tabs, axis=2)


def _attn_b(qkv, qkv_ctx, bias_tabs, bounds, layer, tiles):
    bsz, _, length, _ = qkv.shape
    n_ctx = qkv_ctx.shape[2] // bsz
    tq = min(tiles * NB_QROWS * GRID_W, length)
    assert length % tq == 0
    hps = NB_HEADS_PER_STEP
    return pl.pallas_call(
        functools.partial(_attn_b_kernel, layer=layer),
        grid=(bsz, H_B // hps, length // tq),
        in_specs=[pl.BlockSpec(memory_space=pltpu.SMEM),
                  _query_spec(tq, QB0, hps), _head_spec(length, KB0, hps), _head_spec(length, VB0, hps),
                  _ctx_head_spec(n_ctx, KB0, hps), _ctx_head_spec(n_ctx, VB0, hps),
                  pl.BlockSpec((1, hps) + bias_tabs.shape[2:], lambda b, h, t: (layer, h, 0, 0, 0))],
        out_specs=pl.BlockSpec((1, tq, hps * HEAD_DIM), lambda b, h, t: (b, t, h)),
        out_shape=jax.ShapeDtypeStruct((bsz, length, H_B * HEAD_DIM), _BF16),
        compiler_params=_cparams("parallel", "parallel", "parallel"),
        name="attn_neighbourhood",
    )(bounds, qkv, qkv, qkv, qkv_ctx, qkv_ctx, bias_tabs)


def _sink2_row(sink):
    return jnp.concatenate([sink[g:g + 1, :] for g in range(G_C)], axis=1) * LOG2_E


def _attn_c_kernel(bound_ref, q_ref, k_ref, v_ref, kc_ref, vc_ref, sink_ref, o_ref, *, layer):
    t = pl.program_id(2)
    tq = q_ref.shape[2]
    length = k_ref.shape[2]
    band = 3 * WINDOW
    nq = G_C * WINDOW
    kc = kc_ref[0, 0]
    vc = vc_ref[0, 0]
    sink2 = _sink2_row(sink_ref[0, 0])
    rel = (lax.broadcasted_iota(jnp.int32, (band, nq), 1) % WINDOW
           - lax.broadcasted_iota(jnp.int32, (band, nq), 0))
    units = []
    n_sub = tq // WINDOW
    keep_centred = jnp.abs(rel + WINDOW) <= WINDOW
    for n in range(n_sub):
        i0 = t * tq + n * WINDOW
        start = pl.multiple_of(jnp.clip(i0 - WINDOW, 0, length - band), WINDOW)
        kb = k_ref[0, 0, pl.ds(start, band), :]
        vb = v_ref[0, 0, pl.ds(start, band), :]
        q3 = q_ref[0, :, n * WINDOW:(n + 1) * WINDOW, :].reshape(nq, HEAD_DIM)
        keep = keep_centred if 0 < n < n_sub - 1 else jnp.abs(rel + (i0 - start)) <= WINDOW
        units.append((q3, [(kb, vb, None, keep), (kc, vc, None, None)], sink2))

    def write(outs):
        for n, o_t in enumerate(outs):
            o = o_t.T
            o_ref[0, n * WINDOW:(n + 1) * WINDOW, :] = jnp.concatenate(
                [o[g * WINDOW:(g + 1) * WINDOW] for g in range(G_C)], axis=1).astype(_BF16)

    outs, min_sum = _attend_units_shifted(units, bound_ref[layer, 2])
    write(outs)

    @pl.when(min_sum < MIN_ROW_SUM)
    def _():
        write(_attend_units(units))


def _attn_c(qkv, qkv_ctx, sink_tabs, bounds, layer, tq):
    bsz, _, length, _ = qkv.shape
    n_ctx = qkv_ctx.shape[2] // bsz
    tq = min(tq, length)
    return pl.pallas_call(
        functools.partial(_attn_c_kernel, layer=layer),
        grid=(bsz, KV_C, length // tq),
        in_specs=[pl.BlockSpec(memory_space=pltpu.SMEM),
                  _query_spec(tq, QC0, G_C), _head_spec(length, KC0), _head_spec(length, VC0),
                  _ctx_head_spec(n_ctx, KC0), _ctx_head_spec(n_ctx, VC0),
                  pl.BlockSpec((1, 1, G_C, HEAD_DIM), lambda b, h, t: (layer, h, 0, 0))],
        out_specs=pl.BlockSpec((1, tq, G_C * HEAD_DIM), lambda b, h, t: (b, t, h)),
        out_shape=jax.ShapeDtypeStruct((bsz, length, H_C * HEAD_DIM), _BF16),
        compiler_params=_cparams("parallel", "parallel", "parallel"),
        name="attn_window",
    )(bounds, qkv, qkv, qkv, qkv_ctx, qkv_ctx, sink_tabs)


def _attn_ctx_kernel(qkv_ref, sink_ref, ya_ref, yb_ref, yc_ref):
    n = qkv_ref.shape[2]

    def head(u):
        return qkv_ref[0, u]

    def heads(u0, g):
        return qkv_ref[0, u0:u0 + g].reshape(g * n, HEAD_DIM)

    for kv in range(KV_A):
        o = _attend_t(heads(QA0 + kv * G_A, G_A), [(head(KA0 + kv), head(VA0 + kv), None, None)]).T
        for g in range(G_A):
            u = kv * G_A + g
            ya_ref[0, :, u * HEAD_DIM:(u + 1) * HEAD_DIM] = o[g * n:(g + 1) * n].astype(_BF16)
    for hb in range(H_B):
        o = _attend_t(head(QB0 + hb), [(head(KB0 + hb), head(VB0 + hb), None, None)]).T
        yb_ref[0, :, hb * HEAD_DIM:(hb + 1) * HEAD_DIM] = o.astype(_BF16)
    for kv in range(KV_C):
        sink = sink_ref[0, kv]
        sink2 = jnp.concatenate(
            [jnp.broadcast_to(sink[g:g + 1, :1], (1, n)) for g in range(G_C)], axis=1) * LOG2_E
        o = _attend_t(heads(QC0 + kv * G_C, G_C), [(head(KC0 + kv), head(VC0 + kv), None, None)], sink2).T
        for g in range(G_C):
            u = kv * G_C + g
            yc_ref[0, :, u * HEAD_DIM:(u + 1) * HEAD_DIM] = o[g * n:(g + 1) * n].astype(_BF16)


def _attn_ctx(qkv_ctx, sink_tabs, layer, bsz):
    n_heads, rows = qkv_ctx.shape[1:3]
    n = rows // bsz
    widths = (H_A * HEAD_DIM, H_B * HEAD_DIM, H_C * HEAD_DIM)
    return pl.pallas_call(
        _attn_ctx_kernel,
        grid=(bsz,),
        in_specs=[pl.BlockSpec((1, n_heads, n, HEAD_DIM), lambda b: (0, 0, b, 0)),
                  pl.BlockSpec((1,) + sink_tabs.shape[1:], lambda b: (layer, 0, 0, 0))],
        out_specs=[pl.BlockSpec((1, n, w), lambda b: (0, b, 0)) for w in widths],
        out_shape=[jax.ShapeDtypeStruct((1, rows, w), _BF16) for w in widths],
        compiler_params=_cparams("parallel"),
        name="attn_ctx",
    )(qkv_ctx, sink_tabs)


def _outproj_kernel(ya_ref, yb_ref, yc_ref, x_ref, mod_ref, g_ref, w_ref, xo_ref, h_ref):
    tm, d = x_ref.shape[1:]
    mod = mod_ref[0, 0]
    chunks = [slice(m0, min(m0 + OUTPROJ_ROW_CHUNK, tm)) for m0 in range(0, tm, OUTPROJ_ROW_CHUNK)]

    def project(rows):
        acc = None
        r0 = 0
        for y_ref in (ya_ref, yb_ref, yc_ref):
            r1 = r0 + y_ref.shape[-1]
            part = jnp.dot(y_ref[0, rows, :], w_ref[0, r0:r1, :], preferred_element_type=_F32)
            acc = part if acc is None else acc + part
            r0 = r1
        return acc

    acc_next = project(chunks[0])
    for i, rows in enumerate(chunks):
        acc = acc_next
        if i + 1 < len(chunks):
            acc_next = project(chunks[i + 1])
        xn = x_ref[0, rows, :] + mod[:, 2 * d:3 * d] * acc
        xo_ref[0, rows, :] = xn
        h_ref[0, rows, :] = _norm_mod(xn, g_ref[0], mod[:, 3 * d:4 * d],
                                      mod[:, 4 * d:5 * d]).astype(_BF16)


def _outproj(ya, yb, yc, x, mod, layer, mod_row, gain, w_out, tm):
    bsz, length, d = x.shape
    tm = min(tm, length)
    row = lambda b, i: (b, i, 0)
    return pl.pallas_call(
        _outproj_kernel,
        grid=(bsz, length // tm),
        in_specs=[pl.BlockSpec((1, tm, y.shape[-1]), row) for y in (ya, yb, yc)] + [
            pl.BlockSpec((1, tm, d), row),
            _mod_spec(mod, layer, mod_row, 2),
            pl.BlockSpec((1, 1, d), lambda b, i: (layer, 0, 0)),
            pl.BlockSpec((1,) + w_out.shape[1:], lambda b, i: (layer, 0, 0), pipeline_mode=pl.Buffered(1)),
        ],
        out_specs=[pl.BlockSpec((1, tm, d), row), pl.BlockSpec((1, tm, d), row)],
        out_shape=[jax.ShapeDtypeStruct((bsz, length, d), _F32),
                   jax.ShapeDtypeStruct((bsz, length, d), _BF16)],
        compiler_params=_cparams("parallel", "parallel"),
        name="out_proj_residual",
    )(ya, yb, yc, x, mod, gain, w_out)


def _mlp_kernel(h_ref, x_ref, mod_ref, wu_ref, wd_ref, o_ref):
    k = pl.program_id(2)
    d = x_ref.shape[-1]
    @pl.when(k == 0)
    def _():
        o_ref[...] = jnp.zeros_like(o_ref)

    u = jnp.dot(h_ref[0], wu_ref[0], preferred_element_type=_F32)
    act = jnp.square(jnp.maximum(u, 0.0)).astype(_BF16)
    o_ref[0] += jnp.dot(act, wd_ref[0], preferred_element_type=_F32)

    @pl.when(k == pl.num_programs(2) - 1)
    def _():
        o_ref[0] = x_ref[0] + mod_ref[0, 0, :, 5 * d:6 * d] * o_ref[0]


def _mlp(h, x, mod, layer, mod_row, w_up, w_down, tm, tf):
    bsz, length, d = x.shape
    d_ff = w_up.shape[-1]
    tm = min(tm, length)
    tf = min(tf, d_ff)
    row = lambda b, i, k: (b, i, 0)
    return pl.pallas_call(
        _mlp_kernel,
        grid=(bsz, length // tm, d_ff // tf),
        in_specs=[pl.BlockSpec((1, tm, d), row),
                  pl.BlockSpec((1, tm, d), row),
                  _mod_spec(mod, layer, mod_row, 3),
                  pl.BlockSpec((1, d, tf), lambda b, i, k: (layer, 0, k)),
                  pl.BlockSpec((1, tf, d), lambda b, i, k: (layer, k, 0))],
        out_specs=pl.BlockSpec((1, tm, d), row),
        out_shape=jax.ShapeDtypeStruct((bsz, length, d), _F32),
        compiler_params=_cparams("parallel", "parallel", "arbitrary"),
        name="mlp_sq_relu",
    )(h, x, mod, w_up, w_down)


def _logit_bounds(q_norm, k_norm, rel_pos_bias, sink_logits):
    gq = jnp.max(jnp.abs(q_norm), axis=-1)
    gk = jnp.max(jnp.abs(k_norm), axis=-1)
    qk_bound = HEAD_DIM * LOGIT2_SCALE * BOUND_MARGIN * gq * gk + BOUND_MARGIN - 1.0
    bias_max = jnp.maximum(jnp.max(rel_pos_bias, axis=(1, 2, 3)), 0.0) * LOG2_E
    sink_max = jnp.max(sink_logits, axis=-1) * LOG2_E
    return jnp.stack([qk_bound[:, 0], qk_bound[:, 1] + bias_max,
                      jnp.maximum(qk_bound[:, 2], sink_max)], axis=-1).astype(_F32)


def _rope_tables(length):
    t = np.arange(length)
    nf = HEAD_DIM // 4
    inv = ROPE_THETA ** (-jnp.arange(nf, dtype=_F32) / nf)
    ar = jnp.asarray(t // GRID_W, _F32)[:, None] * inv
    ac = jnp.asarray(t % GRID_W, _F32)[:, None] * inv
    ang = jnp.concatenate([ar, ar, ac, ac], axis=-1)
    cos, sin = jnp.cos(ang), jnp.sin(ang)
    first = (np.arange(HEAD_DIM) // nf) % 2 == 0
    return cos, jnp.where(first, -sin, 0.0), jnp.where(first, 0.0, sin)


def kernel(x, c, ctx, c_ctx, w_mod, b_mod, norm_attn, norm_mlp, w_in, q_norm, k_norm,
           rel_pos_bias, sink_logits, w_out, w_up, w_down):
    bsz, length, d = x.shape
    depth = w_mod.shape[0]
    assert length % (NB_QROWS * GRID_W) == 0 and length // GRID_W >= NB_KROWS
    assert w_in.shape[-1] == N_HEADS_IN * HEAD_DIM and w_out.shape[1] == N_HEADS_OUT * HEAD_DIM

    mod_rows = 8
    c_rows = jnp.zeros((mod_rows, d), _F32).at[:bsz].set(c).at[bsz].set(c_ctx)
    mod = _modulation(c_rows, w_mod, b_mod).reshape(depth, mod_rows, 1, N_MOD * d)

    w_in_b, w_out_b = w_in.astype(_BF16), w_out.astype(_BF16)
    w_up_b, w_down_b = w_up.astype(_BF16), w_down.astype(_BF16)
    g_attn = norm_attn.reshape(depth, 1, d)
    g_mlp = norm_mlp.reshape(depth, 1, d)
    rope_tabs = _rope_tables(length)
    bounds = _logit_bounds(q_norm, k_norm, rel_pos_bias, sink_logits)
    bias_tabs = _nb_bias_tables(rel_pos_bias, length // GRID_W, bounds[:, 1])
    sink_tabs = jnp.broadcast_to(
        sink_logits.reshape(depth, KV_C, G_C, 1), (depth, KV_C, G_C, HEAD_DIM)).astype(_F32)

    cs = ctx.reshape(1, bsz * ctx.shape[1], d)
    for l in range(depth):
        last = l == depth - 1
        qkv, vt = _qkv_proj(x, mod, l, None, g_attn, w_in_b, q_norm, k_norm, rope_tabs, tm=256)
        qkv_c, vt_c = _qkv_proj(cs, mod, l, bsz, g_attn, w_in_b, q_norm, k_norm, None, tm=512)
        ya = _attn_a(qkv, vt, qkv_c, vt_c, bounds, l, tq=1024, tk=1024)
        yb = _attn_b(qkv, qkv_c, bias_tabs, bounds, l, tiles=8)
        yc = _attn_c(qkv, qkv_c, sink_tabs, bounds, l, tq=4096)
        x, h = _outproj(ya, yb, yc, x, mod, l, None, g_mlp, w_out_b, tm=512)
        x = _mlp(h, x, mod, l, None, w_up_b, w_down_b, tm=512, tf=1024)
        if not last:
            ya_c, yb_c, yc_c = _attn_ctx(qkv_c, sink_tabs, l, bsz)
            cs, hc = _outproj(ya_c, yb_c, yc_c, cs, mod, l, bsz, g_mlp, w_out_b, tm=512)
            cs = _mlp(hc, cs, mod, l, bsz, w_up_b, w_down_b, tm=512, tf=1024)
    return x
```

```python
import functools

import numpy as np
import jax
import jax.numpy as jnp
from jax import lax
from jax.experimental import pallas as pl
from jax.experimental.pallas import tpu as pltpu

HEAD_DIM = 128
H_A, KV_A = 4, 2
H_B = 6
H_C, KV_C = 6, 2
G_A = H_A // KV_A
G_C = H_C // KV_C
GRID_W = 64
NA_ROWS = 8
NA_COLS = 16
WINDOW = 128
ROPE_THETA = 10000.0
EPS = 1e-6
N_MOD = 6
ATTN_SCALE = HEAD_DIM ** -0.5
NEG_INF = -1e30
LOG2_E = 1.4426950408889634
LOGIT2_SCALE = ATTN_SCALE * LOG2_E
BOUND_MARGIN = 1.02
MIN_ROW_SUM = 2.0 ** -64

_IN_QA, _IN_KA, _IN_VA = 0, H_A, H_A + KV_A
_IN_QB = _IN_VA + KV_A
_IN_KB, _IN_VB = _IN_QB + H_B, _IN_QB + 2 * H_B
_IN_QC = _IN_VB + H_B
_IN_KC, _IN_VC = _IN_QC + H_C, _IN_QC + H_C + KV_C
N_HEADS_IN = _IN_VC + KV_C
N_HEADS_OUT = H_A + H_B + H_C

QC0 = 0
QA0 = QC0 + H_C
KA0 = QA0 + H_A
VA0 = KA0 + KV_A
QB0 = VA0 + KV_A
KB0 = QB0 + H_B
VB0 = KB0 + H_B
KC0 = VB0 + H_B
VC0 = KC0 + KV_C
assert QA0 % G_A == 0 and QC0 % G_C == 0 and VC0 + KV_C == N_HEADS_IN

_REGIONS = (
    (_IN_QA, QA0, H_A, "q", 0, True), (_IN_KA, KA0, KV_A, "k", 0, True),
    (_IN_QC, QC0, H_C, "q", 2, True), (_IN_KC, KC0, KV_C, "k", 2, True),
    (_IN_QB, QB0, H_B, "q", 1, False), (_IN_KB, KB0, H_B, "k", 1, False),
    (_IN_VA, VA0, KV_A, None, 0, False), (_IN_VC, VC0, KV_C, None, 0, False), (_IN_VB, VB0, H_B, None, 0, False),
)

VT_ROWS = HEAD_DIM + 16
NB_QROWS = 4
NB_KROWS = NB_QROWS + NA_ROWS - 1
NB_HEADS_PER_STEP = 2
OUTPROJ_ROW_CHUNK = 256
VMEM_LIMIT_BYTES = 56 * 1024 * 1024

_BF16 = jnp.bfloat16
_F32 = jnp.float32
_NT = (((1,), (1,)), ((), ()))
_TN = (((0,), (0,)), ((), ()))


def _cparams(*sem):
    return pltpu.CompilerParams(dimension_semantics=sem, vmem_limit_bytes=VMEM_LIMIT_BYTES)


def _norm_mod(x, gain, shift, scale):
    y = x * lax.rsqrt(jnp.mean(x * x, axis=-1, keepdims=True) + EPS)
    return (y * gain) * (1.0 + scale) + shift


def _logits2_t(k, q):
    return lax.dot_general(k, q, _NT, preferred_element_type=_F32)


def _attend_logits(q, kvs):
    zs = []
    for k, _, bias2, keep in kvs:
        z = _logits2_t(k, q)
        if bias2 is not None:
            z = z + bias2
        if keep is not None:
            z = jnp.where(keep, z, NEG_INF)
        zs.append(z)
    return zs


def _attend_t(q, kvs, sink2_row=None, zs=None):
    if zs is None:
        zs = _attend_logits(q, kvs)
    m = zs[0].max(axis=0, keepdims=True)
    for z in zs[1:]:
        m = jnp.maximum(m, z.max(axis=0, keepdims=True))
    if sink2_row is not None:
        m = jnp.maximum(m, sink2_row)
    l = None
    acc = None
    for z, (_, v, _, _) in zip(zs, kvs):
        p = jnp.exp2(z - m)
        ls = p.sum(axis=0, keepdims=True)
        pv = lax.dot_general(v, p.astype(_BF16), _TN, preferred_element_type=_F32)
        l = ls if l is None else l + ls
        acc = pv if acc is None else acc + pv
    if sink2_row is not None:
        l = l + jnp.exp2(sink2_row - m)
    return acc / l


def _attend_units(units):
    outs = []
    zs_next = _attend_logits(*units[0][:2])
    for u, (q, kvs, sink2) in enumerate(units):
        zs = zs_next
        if u + 1 < len(units):
            zs_next = _attend_logits(*units[u + 1][:2])
        outs.append(_attend_t(q, kvs, sink2, zs))
    return outs


def _attend_units_shifted(units, bound):
    outs = []
    min_sum = None
    zs_next = _attend_logits(*units[0][:2])
    for u, (q, kvs, sink2) in enumerate(units):
        zs = zs_next
        if u + 1 < len(units):
            zs_next = _attend_logits(*units[u + 1][:2])
        l = None
        acc = None
        for z, (_, v, _, _) in zip(zs, kvs):
            p = jnp.exp2(z if bound is None else z - bound)
            ls = p.sum(axis=0, keepdims=True)
            pv = lax.dot_general(v, p.astype(_BF16), _TN, preferred_element_type=_F32)
            l = ls if l is None else l + ls
            acc = pv if acc is None else acc + pv
        if sink2 is not None:
            l = l + jnp.exp2(sink2 if bound is None else sink2 - bound)
        outs.append(acc / l)
        min_sum = l if min_sum is None else jnp.minimum(min_sum, l)
    return outs, jnp.min(min_sum)


def _mod_kernel(c_ref, w_ref, b_ref, o_ref):
    c = c_ref[...]
    cond = c * jax.nn.sigmoid(c)
    o_ref[0] = jnp.dot(cond.astype(_BF16), w_ref[0].astype(_BF16),
                       preferred_element_type=_F32) + b_ref[0]


def _modulation(c_rows, w_mod, b_mod):
    depth, d, n = w_mod.shape
    tn = next(t for t in (1024, 512, 256, 128, n) if n % t == 0)
    rows = c_rows.shape[0]
    return pl.pallas_call(
        _mod_kernel,
        grid=(depth, n // tn),
        in_specs=[pl.BlockSpec((rows, d), lambda l, j: (0, 0)),
                  pl.BlockSpec((1, d, tn), lambda l, j: (l, 0, j)),
                  pl.BlockSpec((1, 1, tn), lambda l, j: (l, 0, j))],
        out_specs=pl.BlockSpec((1, rows, tn), lambda l, j: (l, 0, j)),
        out_shape=jax.ShapeDtypeStruct((depth, rows, n), _F32),
        compiler_params=_cparams("parallel", "parallel"),
        name="adaln_modulation",
    )(c_rows, w_mod, b_mod.reshape(depth, 1, n))


def _mod_spec(mod, layer, mod_row, n_grid):
    block = (1, 1, 1, mod.shape[-1])
    if n_grid == 2:
        return pl.BlockSpec(block, (lambda b, i: (layer, b, 0, 0)) if mod_row is None
                            else (lambda b, i: (layer, mod_row, 0, 0)))
    return pl.BlockSpec(block, (lambda b, i, k: (layer, b, 0, 0)) if mod_row is None
                        else (lambda b, i, k: (layer, mod_row, 0, 0)))


def _qkv_kernel(x_ref, mod_ref, g_ref, w_ref, qn_ref, kn_ref, *rest, rope):
    if rope:
        cos_ref, sa_ref, sb_ref, o_ref, vt_ref = rest
    else:
        o_ref, vt_ref = rest
    d = x_ref.shape[-1]
    mod = mod_ref[0, 0]
    h = _norm_mod(x_ref[0], g_ref[0], mod[:, 0:d], mod[:, d:2 * d]).astype(_BF16)
    for c0, o0, nh, kind, row, rotary in _REGIONS:
        y = jnp.dot(h, w_ref[0, :, c0 * HEAD_DIM:(c0 + nh) * HEAD_DIM], preferred_element_type=_F32)
        for j in range(nh):
            yj = y[:, j * HEAD_DIM:(j + 1) * HEAD_DIM]
            if kind is not None:
                gain = (qn_ref if kind == "q" else kn_ref)[0, row:row + 1, :]
                yj = yj * lax.rsqrt(jnp.mean(yj * yj, axis=-1, keepdims=True) + EPS) * gain
                if rope and rotary:
                    yj = (yj * cos_ref[...]
                          + pltpu.roll(yj, HEAD_DIM - HEAD_DIM // 4, 1) * sa_ref[...]
                          + pltpu.roll(yj, HEAD_DIM // 4, 1) * sb_ref[...])
                if kind == "q":
                    yj = yj * LOGIT2_SCALE
            o_ref[0, o0 + j] = yj.astype(_BF16)
            if o0 == VA0:
                vt_ref[0, j, :HEAD_DIM, :] = yj.T.astype(_BF16)
                vt_ref[0, j, HEAD_DIM:, :] = jnp.ones((VT_ROWS - HEAD_DIM, yj.shape[0]), _BF16)


def _qkv_proj(x, mod, layer, mod_row, gain, w_in, qn, kn, rope_tabs, tm):
    bsz, length, d = x.shape
    d_in = w_in.shape[-1]
    tm = min(tm, length)
    in_specs = [
        pl.BlockSpec((1, tm, d), lambda b, i: (b, i, 0)),
        _mod_spec(mod, layer, mod_row, 2),
        pl.BlockSpec((1, 1, d), lambda b, i: (layer, 0, 0)),
        pl.BlockSpec((1, d, d_in), lambda b, i: (layer, 0, 0), pipeline_mode=pl.Buffered(1)),
        pl.BlockSpec((1,) + qn.shape[1:], lambda b, i: (layer, 0, 0)),
        pl.BlockSpec((1,) + kn.shape[1:], lambda b, i: (layer, 0, 0)),
    ]
    args = [x, mod, gain, w_in, qn, kn]
    if rope_tabs is not None:
        in_specs += [pl.BlockSpec((tm, HEAD_DIM), lambda b, i: (i, 0))] * 3
        args += list(rope_tabs)
    return pl.pallas_call(
        functools.partial(_qkv_kernel, rope=rope_tabs is not None),
        grid=(bsz, length // tm),
        in_specs=in_specs,
        out_specs=[pl.BlockSpec((1, N_HEADS_IN, tm, HEAD_DIM), lambda b, i: (b, 0, i, 0)),
                   pl.BlockSpec((1, KV_A, VT_ROWS, tm), lambda b, i: (b, 0, 0, i))],
        out_shape=[jax.ShapeDtypeStruct((bsz, N_HEADS_IN, length, HEAD_DIM), _BF16),
                   jax.ShapeDtypeStruct((bsz, KV_A, VT_ROWS, length), _BF16)],
        compiler_params=_cparams("parallel", "parallel"),
        name="qkv_proj_rope" if rope_tabs is not None else "qkv_proj_ctx",
    )(*args)


def _head_spec(rows, head0, heads=1):
    return pl.BlockSpec((1, heads, rows, HEAD_DIM), lambda b, h, t: (b, head0 // heads + h, 0, 0))


def _ctx_head_spec(rows, head0, heads=1):
    return pl.BlockSpec((1, heads, rows, HEAD_DIM), lambda b, h, t: (0, head0 // heads + h, b, 0))


def _query_spec(rows, head0, heads):
    return pl.BlockSpec((1, heads, rows, HEAD_DIM), lambda b, h, t: (b, head0 // heads + h, t, 0))


def _attn_a_kernel(bound_ref, q_ref, k_ref, vt_ref, kc_ref, vtc_ref, o_ref, acc_sc, s0_sc, s1_sc,
                   *, tk, layer):
    tq = q_ref.shape[2]
    length = k_ref.shape[2]
    nq = G_A * tq
    n_blocks = length // tk
    q2 = q_ref[0].reshape(nq, HEAD_DIM)

    def qk(kb):
        return _logits2_t(kb, q2)

    def keys(ref, j):
        return ref[0, 0, pl.ds(pl.multiple_of(j * tk, tk), tk), :]

    def values_t(j):
        return vt_ref[0, 0, :, pl.ds(pl.multiple_of(j * tk, tk), tk)]

    bound = bound_ref[layer, 0]

    def shifted_pv(kb, vb):
        p = jnp.exp2(qk(kb) - bound)
        pv = jnp.dot(vb[:HEAD_DIM], p.astype(_BF16), preferred_element_type=_F32)
        return pv, p.sum(axis=0, keepdims=True)

    pv, row_sum = shifted_pv(kc_ref[0, 0], vtc_ref[0, 0])
    acc_sc[:HEAD_DIM, :] = pv

    def shifted_block(j, row_sum):
        pv, block_sum = shifted_pv(keys(k_ref, j), values_t(j))
        acc_sc[:HEAD_DIM, :] += pv
        return row_sum + block_sum

    row_sum = lax.fori_loop(0, n_blocks, shifted_block, row_sum, unroll=min(8, n_blocks))
    acc_sc[HEAD_DIM:HEAD_DIM + 1, :] = row_sum

    @pl.when(jnp.min(row_sum) < MIN_ROW_SUM)
    def _():
        _attn_a_running_max(qk, keys, values_t, kc_ref, vtc_ref, k_ref, acc_sc, s0_sc, s1_sc, n_blocks, nq)

    o = (acc_sc[:HEAD_DIM, :] / acc_sc[HEAD_DIM:HEAD_DIM + 1, :]).T
    o_ref[0] = jnp.concatenate([o[g * tq:(g + 1) * tq] for g in range(G_A)], axis=1).astype(_BF16)


def _attn_a_running_max(qk, keys, values_t, kc_ref, vtc_ref, k_ref, acc_sc, s0_sc, s1_sc, n_blocks, nq):
    def softmax_pv(s, vb, m_prev):
        m_new = jnp.maximum(m_prev, s.max(axis=0, keepdims=True))
        alpha = jnp.exp2(m_prev - m_new)
        p = jnp.exp2(s - m_new)
        pv = jnp.dot(vb, p.astype(_BF16), preferred_element_type=_F32)
        acc_sc[...] = alpha * acc_sc[...] + pv
        return m_new

    acc_sc[...] = jnp.zeros_like(acc_sc)
    m0 = jnp.full((1, nq), NEG_INF, _F32)
    s0_sc[...] = qk(keys(k_ref, 0))
    m = softmax_pv(qk(kc_ref[0, 0]), vtc_ref[0, 0], m0)

    def pair(j0, m, last):
        s1_sc[...] = qk(keys(k_ref, j0 + 1))
        m = softmax_pv(s0_sc[...], values_t(j0), m)
        if not last:
            s0_sc[...] = qk(keys(k_ref, j0 + 2))
        return softmax_pv(s1_sc[...], values_t(j0 + 1), m)

    m = lax.fori_loop(0, n_blocks // 2 - 1, lambda jj, m: pair(2 * jj, m, False), m)
    pair(n_blocks - 2, m, True)


def _attn_a(qkv, vt, qkv_ctx, vt_ctx, bounds, layer, tq, tk):
    bsz, _, length, _ = qkv.shape
    n_ctx = qkv_ctx.shape[2] // bsz
    tq = min(tq, length)
    tk = min(tk, length // 2)
    assert length % (2 * tk) == 0
    return pl.pallas_call(
        functools.partial(_attn_a_kernel, tk=tk, layer=layer),
        grid=(bsz, KV_A, length // tq),
        in_specs=[pl.BlockSpec(memory_space=pltpu.SMEM),
                  _query_spec(tq, QA0, G_A), _head_spec(length, KA0),
                  pl.BlockSpec((1, 1, VT_ROWS, length), lambda b, h, i: (b, h, 0, 0)),
                  _ctx_head_spec(n_ctx, KA0),
                  pl.BlockSpec((1, 1, VT_ROWS, n_ctx), lambda b, h, i: (0, h, 0, b))],
        out_specs=pl.BlockSpec((1, tq, G_A * HEAD_DIM), lambda b, h, i: (b, i, h)),
        out_shape=jax.ShapeDtypeStruct((bsz, length, H_A * HEAD_DIM), _BF16),
        scratch_shapes=[pltpu.VMEM((VT_ROWS, G_A * tq), _F32),
                        pltpu.VMEM((tk, G_A * tq), _F32), pltpu.VMEM((tk, G_A * tq), _F32)],
        compiler_params=_cparams("parallel", "parallel", "parallel"),
        name="attn_global",
    )(bounds, qkv, qkv, vt, qkv_ctx, vt_ctx)


def _attn_b_kernel(bound_ref, q_ref, k_ref, v_ref, kc_ref, vc_ref, bias_ref, o_ref, *, layer):
    n_rows = k_ref.shape[2] // GRID_W
    tq = NB_QROWS * GRID_W
    nk = NB_KROWS * GRID_W
    tiles = q_ref.shape[2] // tq
    n_tiles = pl.num_programs(2) * tiles
    bound = bound_ref[layer, 1]
    units = []
    for i in range(tiles):
        t = pl.program_id(2) * tiles + i
        ks = jnp.clip(t * NB_QROWS - NA_ROWS // 2, 0, n_rows - NB_KROWS)
        off = pl.multiple_of(ks * GRID_W, GRID_W)
        variant = jnp.where(t == 0, 0, jnp.where(t == n_tiles - 1, 2, 1))
        for hh in range(NB_HEADS_PER_STEP):
            kvs = [(k_ref[0, hh, pl.ds(off, nk), :], v_ref[0, hh, pl.ds(off, nk), :],
                    bias_ref[0, hh, variant], None),
                   (kc_ref[0, hh], vc_ref[0, hh], -bound, None)]
            units.append((q_ref[0, hh, i * tq:(i + 1) * tq, :], kvs, None))

    def write(outs):
        for i in range(tiles):
            heads = outs[i * NB_HEADS_PER_STEP:(i + 1) * NB_HEADS_PER_STEP]
            o_ref[0, i * tq:(i + 1) * tq, :] = jnp.concatenate([o.T for o in heads], axis=1).astype(_BF16)

    outs, min_sum = _attend_units_shifted(units, None)
    write(outs)

    @pl.when(min_sum < MIN_ROW_SUM)
    def _():
        write(_attend_units(units))


def _nb_bias_tables(rpb, n_rows, shift):
    kh = NA_ROWS
    c = np.arange(GRID_W)[:, None, None]
    j = np.arange(2 * NA_COLS - 1)[None, :, None]
    ck = np.arange(GRID_W)[None, None, :]
    col_onehot = (j == ck - c + (NA_COLS - 1)).astype(np.float32)
    cs = np.clip(c - NA_COLS // 2, 0, GRID_W - NA_COLS)
    col_ok = ((ck >= cs) & (ck < cs + NA_COLS))[:, 0, :]
    blocks = jnp.einsum("lhrj,cjk->lhrkc", rpb * LOG2_E, col_onehot, precision=lax.Precision.HIGHEST)
    blocks = blocks - shift[:, None, None, None, None]
    blocks = jnp.where(col_ok.T, blocks, NEG_INF).astype(_F32)
    masked = jnp.full(blocks.shape[:2] + (GRID_W, GRID_W), NEG_INF, _F32)
    tabs = []
    for r0, ks in ((0, 0), (NB_QROWS, 0), (n_rows - NB_QROWS, n_rows - NB_KROWS)):
        key_rows = []
        for i in range(NB_KROWS):
            row = []
            for a in range(NB_QROWS):
                rq, rk = r0 + a, ks + i
                rs = min(max(rq - kh // 2, 0), n_rows - kh)
                row.append(blocks[:, :, rk - rq + (NA_ROWS - 1)] if rs <= rk < rs + kh else masked)
            key_rows.append(jnp.concatenate(row, axis=-1))
        tabs.append(jnp.concatenate(key_rows, axis=-2))
    return jnp.stack(tabs, axis=2)


def _attn_b(qkv, qkv_ctx, bias_tabs, bounds, layer, tiles):
    bsz, _, length, _ = qkv.shape
    n_ctx = qkv_ctx.shape[2] // bsz
    tq = min(tiles * NB_QROWS * GRID_W, length)
    assert length % tq == 0
    hps = NB_HEADS_PER_STEP
    return pl.pallas_call(
        functools.partial(_attn_b_kernel, layer=layer),
        grid=(bsz, H_B // hps, length // tq),
        in_specs=[pl.BlockSpec(memory_space=pltpu.SMEM),
                  _query_spec(tq, QB0, hps), _head_spec(length, KB0, hps), _head_spec(length, VB0, hps),
                  _ctx_head_spec(n_ctx, KB0, hps), _ctx_head_spec(n_ctx, VB0, hps),
                  pl.BlockSpec((1, hps) + bias_tabs.shape[2:], lambda b, h, t: (layer, h, 0, 0, 0))],
        out_specs=pl.BlockSpec((1, tq, hps * HEAD_DIM), lambda b, h, t: (b, t, h)),
        out_shape=jax.ShapeDtypeStruct((bsz, length, H_B * HEAD_DIM), _BF16),
        compiler_params=_cparams("parallel", "parallel", "parallel"),
        name="attn_neighbourhood",
    )(bounds, qkv, qkv, qkv, qkv_ctx, qkv_ctx, bias_tabs)


def _sink2_row(sink):
    return jnp.concatenate([sink[g:g + 1, :] for g in range(G_C)], axis=1) * LOG2_E


def _attn_c_kernel(bound_ref, q_ref, k_ref, v_ref, kc_ref, vc_ref, sink_ref, o_ref, *, layer):
    t = pl.program_id(2)
    tq = q_ref.shape[2]
    length = k_ref.shape[2]
    band = 3 * WINDOW
    nq = G_C * WINDOW
    kc = kc_ref[0, 0]
    vc = vc_ref[0, 0]
    sink2 = _sink2_row(sink_ref[0, 0])
    rel = (lax.broadcasted_iota(jnp.int32, (band, nq), 1) % WINDOW
           - lax.broadcasted_iota(jnp.int32, (band, nq), 0))
    units = []
    n_sub = tq // WINDOW
    keep_centred = jnp.abs(rel + WINDOW) <= WINDOW
    for n in range(n_sub):
        i0 = t * tq + n * WINDOW
        start = pl.multiple_of(jnp.clip(i0 - WINDOW, 0, length - band), WINDOW)
        kb = k_ref[0, 0, pl.ds(start, band), :]
        vb = v_ref[0, 0, pl.ds(start, band), :]
        q3 = q_ref[0, :, n * WINDOW:(n + 1) * WINDOW, :].reshape(nq, HEAD_DIM)
        keep = keep_centred if 0 < n < n_sub - 1 else jnp.abs(rel + (i0 - start)) <= WINDOW
        units.append((q3, [(kb, vb, None, keep), (kc, vc, None, None)], sink2))

    def write(outs):
        for n, o_t in enumerate(outs):
            o = o_t.T
            o_ref[0, n * WINDOW:(n + 1) * WINDOW, :] = jnp.concatenate(
                [o[g * WINDOW:(g + 1) * WINDOW] for g in range(G_C)], axis=1).astype(_BF16)

    outs, min_sum = _attend_units_shifted(units, bound_ref[layer, 2])
    write(outs)

    @pl.when(min_sum < MIN_ROW_SUM)
    def _():
        write(_attend_units(units))


def _attn_c(qkv, qkv_ctx, sink_tabs, bounds, layer, tq):
    bsz, _, length, _ = qkv.shape
    n_ctx = qkv_ctx.shape[2] // bsz
    tq = min(tq, length)
    return pl.pallas_call(
        functools.partial(_attn_c_kernel, layer=layer),
        grid=(bsz, KV_C, length // tq),
        in_specs=[pl.BlockSpec(memory_space=pltpu.SMEM),
                  _query_spec(tq, QC0, G_C), _head_spec(length, KC0), _head_spec(length, VC0),
                  _ctx_head_spec(n_ctx, KC0), _ctx_head_spec(n_ctx, VC0),
                  pl.BlockSpec((1, 1, G_C, HEAD_DIM), lambda b, h, t: (layer, h, 0, 0))],
        out_specs=pl.BlockSpec((1, tq, G_C * HEAD_DIM), lambda b, h, t: (b, t, h)),
        out_shape=jax.ShapeDtypeStruct((bsz, length, H_C * HEAD_DIM), _BF16),
        compiler_params=_cparams("parallel", "parallel", "parallel"),
        name="attn_window",
    )(bounds, qkv, qkv, qkv, qkv_ctx, qkv_ctx, sink_tabs)


def _attn_ctx_kernel(qkv_ref, sink_ref, ya_ref, yb_ref, yc_ref):
    n = qkv_ref.shape[2]

    def head(u):
        return qkv_ref[0, u]

    def heads(u0, g):
        return qkv_ref[0, u0:u0 + g].reshape(g * n, HEAD_DIM)

    for kv in range(KV_A):
        o = _attend_t(heads(QA0 + kv * G_A, G_A), [(head(KA0 + kv), head(VA0 + kv), None, None)]).T
        for g in range(G_A):
            u = kv * G_A + g
            ya_ref[0, :, u * HEAD_DIM:(u + 1) * HEAD_DIM] = o[g * n:(g + 1) * n].astype(_BF16)
    for hb in range(H_B):
        o = _attend_t(head(QB0 + hb), [(head(KB0 + hb), head(VB0 + hb), None, None)]).T
        yb_ref[0, :, hb * HEAD_DIM:(hb + 1) * HEAD_DIM] = o.astype(_BF16)
    for kv in range(KV_C):
        sink = sink_ref[0, kv]
        sink2 = jnp.concatenate(
            [jnp.broadcast_to(sink[g:g + 1, :1], (1, n)) for g in range(G_C)], axis=1) * LOG2_E
        o = _attend_t(heads(QC0 + kv * G_C, G_C), [(head(KC0 + kv), head(VC0 + kv), None, None)], sink2).T
        for g in range(G_C):
            u = kv * G_C + g
            yc_ref[0, :, u * HEAD_DIM:(u + 1) * HEAD_DIM] = o[g * n:(g + 1) * n].astype(_BF16)


def _attn_ctx(qkv_ctx, sink_tabs, layer, bsz):
    n_heads, rows = qkv_ctx.shape[1:3]
    n = rows // bsz
    widths = (H_A * HEAD_DIM, H_B * HEAD_DIM, H_C * HEAD_DIM)
    return pl.pallas_call(
        _attn_ctx_kernel,
        grid=(bsz,),
        in_specs=[pl.BlockSpec((1, n_heads, n, HEAD_DIM), lambda b: (0, 0, b, 0)),
                  pl.BlockSpec((1,) + sink_tabs.shape[1:], lambda b: (layer, 0, 0, 0))],
        out_specs=[pl.BlockSpec((1, n, w), lambda b: (0, b, 0)) for w in widths],
        out_shape=[jax.ShapeDtypeStruct((1, rows, w), _BF16) for w in widths],
        compiler_params=_cparams("parallel"),
        name="attn_ctx",
    )(qkv_ctx, sink_tabs)


def _outproj_kernel(ya_ref, yb_ref, yc_ref, x_ref, mod_ref, g_ref, w_ref, xo_ref, h_ref):
    tm, d = x_ref.shape[1:]
    mod = mod_ref[0, 0]
    chunks = [slice(m0, min(m0 + OUTPROJ_ROW_CHUNK, tm)) for m0 in range(0, tm, OUTPROJ_ROW_CHUNK)]

    def project(rows):
        acc = None
        r0 = 0
        for y_ref in (ya_ref, yb_ref, yc_ref):
            r1 = r0 + y_ref.shape[-1]
            part = jnp.dot(y_ref[0, rows, :], w_ref[0, r0:r1, :], preferred_element_type=_F32)
            acc = part if acc is None else acc + part
            r0 = r1
        return acc

    acc_next = project(chunks[0])
    for i, rows in enumerate(chunks):
        acc = acc_next
        if i + 1 < len(chunks):
            acc_next = project(chunks[i + 1])
        xn = x_ref[0, rows, :] + mod[:, 2 * d:3 * d] * acc
        xo_ref[0, rows, :] = xn
        h_ref[0, rows, :] = _norm_mod(xn, g_ref[0], mod[:, 3 * d:4 * d],
                                      mod[:, 4 * d:5 * d]).astype(_BF16)


def _outproj(ya, yb, yc, x, mod, layer, mod_row, gain, w_out, tm):
    bsz, length, d = x.shape
    tm = min(tm, length)
    row = lambda b, i: (b, i, 0)
    return pl.pallas_call(
        _outproj_kernel,
        grid=(bsz, length // tm),
        in_specs=[pl.BlockSpec((1, tm, y.shape[-1]), row) for y in (ya, yb, yc)] + [
            pl.BlockSpec((1, tm, d), row),
            _mod_spec(mod, layer, mod_row, 2),
            pl.BlockSpec((1, 1, d), lambda b, i: (layer, 0, 0)),
            pl.BlockSpec((1,) + w_out.shape[1:], lambda b, i: (layer, 0, 0), pipeline_mode=pl.Buffered(1)),
        ],
        out_specs=[pl.BlockSpec((1, tm, d), row), pl.BlockSpec((1, tm, d), row)],
        out_shape=[jax.ShapeDtypeStruct((bsz, length, d), _F32),
                   jax.ShapeDtypeStruct((bsz, length, d), _BF16)],
        compiler_params=_cparams("parallel", "parallel"),
        name="out_proj_residual",
    )(ya, yb, yc, x, mod, gain, w_out)


def _mlp_kernel(h_ref, x_ref, mod_ref, wu_ref, wd_ref, o_ref):
    k = pl.program_id(2)
    d = x_ref.shape[-1]
    @pl.when(k == 0)
    def _():
        o_ref[...] = jnp.zeros_like(o_ref)

    u = jnp.dot(h_ref[0], wu_ref[0], preferred_element_type=_F32)
    act = jnp.square(jnp.maximum(u, 0.0)).astype(_BF16)
    o_ref[0] += jnp.dot(act, wd_ref[0], preferred_element_type=_F32)

    @pl.when(k == pl.num_programs(2) - 1)
    def _():
        o_ref[0] = x_ref[0] + mod_ref[0, 0, :, 5 * d:6 * d] * o_ref[0]


def _mlp(h, x, mod, layer, mod_row, w_up, w_down, tm, tf):
    bsz, length, d = x.shape
    d_ff = w_up.shape[-1]
    tm = min(tm, length)
    tf = min(tf, d_ff)
    row = lambda b, i, k: (b, i, 0)
    return pl.pallas_call(
        _mlp_kernel,
        grid=(bsz, length // tm, d_ff // tf),
        in_specs=[pl.BlockSpec((1, tm, d), row),
                  pl.BlockSpec((1, tm, d), row),
                  _mod_spec(mod, layer, mod_row, 3),
                  pl.BlockSpec((1, d, tf), lambda b, i, k: (layer, 0, k)),
                  pl.BlockSpec((1, tf, d), lambda b, i, k: (layer, k, 0))],
        out_specs=pl.BlockSpec((1, tm, d), row),
        out_shape=jax.ShapeDtypeStruct((bsz, length, d), _F32),
        compiler_params=_cparams("parallel", "parallel", "arbitrary"),
        name="mlp_sq_relu",
    )(h, x, mod, w_up, w_down)


def _logit_bounds(q_norm, k_norm, rel_pos_bias, sink_logits):
    gq = jnp.max(jnp.abs(q_norm), axis=-1)
    gk = jnp.max(jnp.abs(k_norm), axis=-1)
    qk_bound = HEAD_DIM * LOGIT2_SCALE * BOUND_MARGIN * gq * gk + BOUND_MARGIN - 1.0
    bias_max = jnp.maximum(jnp.max(rel_pos_bias, axis=(1, 2, 3)), 0.0) * LOG2_E
    sink_max = jnp.max(sink_logits, axis=-1) * LOG2_E
    return jnp.stack([qk_bound[:, 0], qk_bound[:, 1] + bias_max,
                      jnp.maximum(qk_bound[:, 2], sink_max)], axis=-1).astype(_F32)


def _rope_tables(length):
    t = np.arange(length)
    nf = HEAD_DIM // 4
    inv = ROPE_THETA ** (-jnp.arange(nf, dtype=_F32) / nf)
    ar = jnp.asarray(t // GRID_W, _F32)[:, None] * inv
    ac = jnp.asarray(t % GRID_W, _F32)[:, None] * inv
    ang = jnp.concatenate([ar, ar, ac, ac], axis=-1)
    cos, sin = jnp.cos(ang), jnp.sin(ang)
    first = (np.arange(HEAD_DIM) // nf) % 2 == 0
    return cos, jnp.where(first, -sin, 0.0), jnp.where(first, 0.0, sin)


def kernel(x, c, ctx, c_ctx, w_mod, b_mod, norm_attn, norm_mlp, w_in, q_norm, k_norm,
           rel_pos_bias, sink_logits, w_out, w_up, w_down):
    bsz, length, d = x.shape
    depth = w_mod.shape[0]
    assert length % (NB_QROWS * GRID_W) == 0 and length // GRID_W >= NB_KROWS
    assert w_in.shape[-1] == N_HEADS_IN * HEAD_DIM and w_out.shape[1] == N_HEADS_OUT * HEAD_DIM

    mod_rows = 8
    c_rows = jnp.zeros((mod_rows, d), _F32).at[:bsz].set(c).at[bsz].set(c_ctx)
    mod = _modulation(c_rows, w_mod, b_mod).reshape(depth, mod_rows, 1, N_MOD * d)

    w_in_b, w_out_b = w_in.astype(_BF16), w_out.astype(_BF16)
    w_up_b, w_down_b = w_up.astype(_BF16), w_down.astype(_BF16)
    g_attn = norm_attn.reshape(depth, 1, d)
    g_mlp = norm_mlp.reshape(depth, 1, d)
    rope_tabs = _rope_tables(length)
    bounds = _logit_bounds(q_norm, k_norm, rel_pos_bias, sink_logits)
    bias_tabs = _nb_bias_tables(rel_pos_bias, length // GRID_W, bounds[:, 1])
    sink_tabs = jnp.broadcast_to(
        sink_logits.reshape(depth, KV_C, G_C, 1), (depth, KV_C, G_C, HEAD_DIM)).astype(_F32)

    cs = ctx.reshape(1, bsz * ctx.shape[1], d)
    for l in range(depth):
        last = l == depth - 1
        qkv, vt = _qkv_proj(x, mod, l, None, g_attn, w_in_b, q_norm, k_norm, rope_tabs, tm=256)
        qkv_c, vt_c = _qkv_proj(cs, mod, l, bsz, g_attn, w_in_b, q_norm, k_norm, None, tm=512)
        ya = _attn_a(qkv, vt, qkv_c, vt_c, bounds, l, tq=1024, tk=1024)
        yb = _attn_b(qkv, qkv_c, bias_tabs, bounds, l, tiles=8)
        yc = _attn_c(qkv, qkv_c, sink_tabs, bounds, l, tq=4096)
        x, h = _outproj(ya, yb, yc, x, mod, l, None, g_mlp, w_out_b, tm=512)
        x = _mlp(h, x, mod, l, None, w_up_b, w_down_b, tm=512, tf=1024)
        if not last:
            ya_c, yb_c, yc_c = _attn_ctx(qkv_c, sink_tabs, l, bsz)
            cs, hc = _outproj(ya_c, yb_c, yc_c, cs, mod, l, bsz, g_mlp, w_out_b, tm=512)
            cs = _mlp(hc, cs, mod, l, bsz, w_up_b, w_down_b, tm=512, tf=1024)
    return x
```

```python
import functools

import numpy as np
import jax
import jax.numpy as jnp
from jax import lax
from jax.experimental import pallas as pl
from jax.experimental.pallas import tpu as pltpu

HEAD_DIM = 128
H_A, KV_A = 4, 2
H_B = 6
H_C, KV_C = 6, 2
G_A = H_A // KV_A
G_C = H_C // KV_C
GRID_W = 64
NA_ROWS = 8
NA_COLS = 16
WINDOW = 128
ROPE_THETA = 10000.0
EPS = 1e-6
N_MOD = 6
ATTN_SCALE = HEAD_DIM ** -0.5
NEG_INF = -1e30
LOG2_E = 1.4426950408889634
LOGIT2_SCALE = ATTN_SCALE * LOG2_E
BOUND_MARGIN = 1.02
MIN_ROW_SUM = 2.0 ** -64

_IN_QA, _IN_KA, _IN_VA = 0, H_A, H_A + KV_A
_IN_QB = _IN_VA + KV_A
_IN_KB, _IN_VB = _IN_QB + H_B, _IN_QB + 2 * H_B
_IN_QC = _IN_VB + H_B
_IN_KC, _IN_VC = _IN_QC + H_C, _IN_QC + H_C + KV_C
N_HEADS_IN = _IN_VC + KV_C
N_HEADS_OUT = H_A + H_B + H_C

QC0 = 0
QA0 = QC0 + H_C
KA0 = QA0 + H_A
VA0 = KA0 + KV_A
QB0 = VA0 + KV_A
KB0 = QB0 + H_B
VB0 = KB0 + H_B
KC0 = VB0 + H_B
VC0 = KC0 + KV_C
assert QA0 % G_A == 0 and QC0 % G_C == 0 and VC0 + KV_C == N_HEADS_IN

_REGIONS = (
    (_IN_QA, QA0, H_A, "q", 0, True), (_IN_KA, KA0, KV_A, "k", 0, True),
    (_IN_QC, QC0, H_C, "q", 2, True), (_IN_KC, KC0, KV_C, "k", 2, True),
    (_IN_QB, QB0, H_B, "q", 1, False), (_IN_KB, KB0, H_B, "k", 1, False),
    (_IN_VA, VA0, KV_A, None, 0, False), (_IN_VC, VC0, KV_C, None, 0, False), (_IN_VB, VB0, H_B, None, 0, False),
)

VT_ROWS = HEAD_DIM + 16
NB_QROWS = 4
NB_KROWS = NB_QROWS + NA_ROWS - 1
NB_HEADS_PER_STEP = 2
OUTPROJ_ROW_CHUNK = 256
VMEM_LIMIT_BYTES = 56 * 1024 * 1024

_BF16 = jnp.bfloat16
_F32 = jnp.float32
_NT = (((1,), (1,)), ((), ()))
_TN = (((0,), (0,)), ((), ()))


def _cparams(*sem):
    return pltpu.CompilerParams(dimension_semantics=sem, vmem_limit_bytes=VMEM_LIMIT_BYTES)


def _norm_mod(x, gain, shift, scale):
    y = x * lax.rsqrt(jnp.mean(x * x, axis=-1, keepdims=True) + EPS)
    return (y * gain) * (1.0 + scale) + shift


def _logits2_t(k, q):
    return lax.dot_general(k, q, _NT, preferred_element_type=_F32)


def _attend_logits(q, kvs):
    zs = []
    for k, _, bias2, keep in kvs:
        z = _logits2_t(k, q)
        if bias2 is not None:
            z = z + bias2
        if keep is not None:
            z = jnp.where(keep, z, NEG_INF)
        zs.append(z)
    return zs


def _attend_t(q, kvs, sink2_row=None, zs=None):
    if zs is None:
        zs = _attend_logits(q, kvs)
    m = zs[0].max(axis=0, keepdims=True)
    for z in zs[1:]:
        m = jnp.maximum(m, z.max(axis=0, keepdims=True))
    if sink2_row is not None:
        m = jnp.maximum(m, sink2_row)
    l = None
    acc = None
    for z, (_, v, _, _) in zip(zs, kvs):
        p = jnp.exp2(z - m)
        ls = p.sum(axis=0, keepdims=True)
        pv = lax.dot_general(v, p.astype(_BF16), _TN, preferred_element_type=_F32)
        l = ls if l is None else l + ls
        acc = pv if acc is None else acc + pv
    if sink2_row is not None:
        l = l + jnp.exp2(sink2_row - m)
    return acc / l


def _attend_units(units):
    outs = []
    zs_next = _attend_logits(*units[0][:2])
    for u, (q, kvs, sink2) in enumerate(units):
        zs = zs_next
        if u + 1 < len(units):
            zs_next = _attend_logits(*units[u + 1][:2])
        outs.append(_attend_t(q, kvs, sink2, zs))
    return outs


def _attend_units_shifted(units, bound):
    outs = []
    min_sum = None
    zs_next = _attend_logits(*units[0][:2])
    for u, (q, kvs, sink2) in enumerate(units):
        zs = zs_next
        if u + 1 < len(units):
            zs_next = _attend_logits(*units[u + 1][:2])
        l = None
        acc = None
        for z, (_, v, _, _) in zip(zs, kvs):
            p = jnp.exp2(z if bound is None else z - bound)
            ls = p.sum(axis=0, keepdims=True)
            pv = lax.dot_general(v, p.astype(_BF16), _TN, preferred_element_type=_F32)
            l = ls if l is None else l + ls
            acc = pv if acc is None else acc + pv
        if sink2 is not None:
            l = l + jnp.exp2(sink2 if bound is None else sink2 - bound)
        outs.append(acc / l)
        min_sum = l if min_sum is None else jnp.minimum(min_sum, l)
    return outs, jnp.min(min_sum)


def _mod_kernel(c_ref, w_ref, b_ref, o_ref):
    c = c_ref[...]
    cond = c * jax.nn.sigmoid(c)
    o_ref[0] = jnp.dot(cond.astype(_BF16), w_ref[0].astype(_BF16),
                       preferred_element_type=_F32) + b_ref[0]


def _modulation(c_rows, w_mod, b_mod):
    depth, d, n = w_mod.shape
    tn = next(t for t in (1024, 512, 256, 128, n) if n % t == 0)
    rows = c_rows.shape[0]
    return pl.pallas_call(
        _mod_kernel,
        grid=(depth, n // tn),
        in_specs=[pl.BlockSpec((rows, d), lambda l, j: (0, 0)),
                  pl.BlockSpec((1, d, tn), lambda l, j: (l, 0, j)),
                  pl.BlockSpec((1, 1, tn), lambda l, j: (l, 0, j))],
        out_specs=pl.BlockSpec((1, rows, tn), lambda l, j: (l, 0, j)),
        out_shape=jax.ShapeDtypeStruct((depth, rows, n), _F32),
        compiler_params=_cparams("parallel", "parallel"),
        name="adaln_modulation",
    )(c_rows, w_mod, b_mod.reshape(depth, 1, n))


def _mod_spec(mod, layer, mod_row, n_grid):
    block = (1, 1, 1, mod.shape[-1])
    if n_grid == 2:
        return pl.BlockSpec(block, (lambda b, i: (layer, b, 0, 0)) if mod_row is None
                            else (lambda b, i: (layer, mod_row, 0, 0)))
    return pl.BlockSpec(block, (lambda b, i, k: (layer, b, 0, 0)) if mod_row is None
                        else (lambda b, i, k: (layer, mod_row, 0, 0)))


def _qkv_kernel(x_ref, mod_ref, g_ref, w_ref, qn_ref, kn_ref, *rest, rope):
    if rope:
        cos_ref, sa_ref, sb_ref, o_ref, vt_ref = rest
    else:
        o_ref, vt_ref = rest
    d = x_ref.shape[-1]
    mod = mod_ref[0, 0]
    h = _norm_mod(x_ref[0], g_ref[0], mod[:, 0:d], mod[:, d:2 * d]).astype(_BF16)
    for c0, o0, nh, kind, row, rotary in _REGIONS:
        y = jnp.dot(h, w_ref[0, :, c0 * HEAD_DIM:(c0 + nh) * HEAD_DIM], preferred_element_type=_F32)
        for j in range(nh):
            yj = y[:, j * HEAD_DIM:(j + 1) * HEAD_DIM]
            if kind is not None:
                gain = (qn_ref if kind == "q" else kn_ref)[0, row:row + 1, :]
                yj = yj * lax.rsqrt(jnp.mean(yj * yj, axis=-1, keepdims=True) + EPS) * gain
                if rope and rotary:
                    yj = (yj * cos_ref[...]
                          + pltpu.roll(yj, HEAD_DIM - HEAD_DIM // 4, 1) * sa_ref[...]
                          + pltpu.roll(yj, HEAD_DIM // 4, 1) * sb_ref[...])
                if kind == "q":
                    yj = yj * LOGIT2_SCALE
            o_ref[0, o0 + j] = yj.astype(_BF16)
            if o0 == VA0:
                vt_ref[0, j, :HEAD_DIM, :] = yj.T.astype(_BF16)
                vt_ref[0, j, HEAD_DIM:, :] = jnp.ones((VT_ROWS - HEAD_DIM, yj.shape[0]), _BF16)


def _qkv_proj(x, mod, layer, mod_row, gain, w_in, qn, kn, rope_tabs, tm):
    bsz, length, d = x.shape
    d_in = w_in.shape[-1]
    tm = min(tm, length)
    in_specs = [
        pl.BlockSpec((1, tm, d), lambda b, i: (b, i, 0)),
        _mod_spec(mod, layer, mod_row, 2),
        pl.BlockSpec((1, 1, d), lambda b, i: (layer, 0, 0)),
        pl.BlockSpec((1, d, d_in), lambda b, i: (layer, 0, 0), pipeline_mode=pl.Buffered(1)),
        pl.BlockSpec((1,) + qn.shape[1:], lambda b, i: (layer, 0, 0)),
        pl.BlockSpec((1,) + kn.shape[1:], lambda b, i: (layer, 0, 0)),
    ]
    args = [x, mod, gain, w_in, qn, kn]
    if rope_tabs is not None:
        in_specs += [pl.BlockSpec((tm, HEAD_DIM), lambda b, i: (i, 0))] * 3
        args += list(rope_tabs)
    return pl.pallas_call(
        functools.partial(_qkv_kernel, rope=rope_tabs is not None),
        grid=(bsz, length // tm),
        in_specs=in_specs,
        out_specs=[pl.BlockSpec((1, N_HEADS_IN, tm, HEAD_DIM), lambda b, i: (b, 0, i, 0)),
                   pl.BlockSpec((1, KV_A, VT_ROWS, tm), lambda b, i: (b, 0, 0, i))],
        out_shape=[jax.ShapeDtypeStruct((bsz, N_HEADS_IN, length, HEAD_DIM), _BF16),
                   jax.ShapeDtypeStruct((bsz, KV_A, VT_ROWS, length), _BF16)],
        compiler_params=_cparams("parallel", "parallel"),
        name="qkv_proj_rope" if rope_tabs is not None else "qkv_proj_ctx",
    )(*args)


def _head_spec(rows, head0, heads=1):
    return pl.BlockSpec((1, heads, rows, HEAD_DIM), lambda b, h, t: (b, head0 // heads + h, 0, 0))


def _ctx_head_spec(rows, head0, heads=1):
    return pl.BlockSpec((1, heads, rows, HEAD_DIM), lambda b, h, t: (0, head0 // heads + h, b, 0))


def _query_spec(rows, head0, heads):
    return pl.BlockSpec((1, heads, rows, HEAD_DIM), lambda b, h, t: (b, head0 // heads + h, t, 0))


def _attn_a_kernel(bound_ref, q_ref, k_ref, vt_ref, kc_ref, vtc_ref, o_ref, acc_sc, s0_sc, s1_sc,
                   *, tk, layer):
    tq = q_ref.shape[2]
    length = k_ref.shape[2]
    nq = G_A * tq
    n_blocks = length // tk
    q2 = q_ref[0].reshape(nq, HEAD_DIM)

    def qk(kb):
        return _logits2_t(kb, q2)

    def keys(ref, j):
        return ref[0, 0, pl.ds(pl.multiple_of(j * tk, tk), tk), :]

    def values_t(j):
        return vt_ref[0, 0, :, pl.ds(pl.multiple_of(j * tk, tk), tk)]

    bound = bound_ref[layer, 0]

    def shifted_pv(kb, vb):
        p = jnp.exp2(qk(kb) - bound)
        pv = jnp.dot(vb[:HEAD_DIM], p.astype(_BF16), preferred_element_type=_F32)
        return pv, p.sum(axis=0, keepdims=True)

    pv, row_sum = shifted_pv(kc_ref[0, 0], vtc_ref[0, 0])
    acc_sc[:HEAD_DIM, :] = pv

    def shifted_block(j, row_sum):
        pv, block_sum = shifted_pv(keys(k_ref, j), values_t(j))
        acc_sc[:HEAD_DIM, :] += pv
        return row_sum + block_sum

    row_sum = lax.fori_loop(0, n_blocks, shifted_block, row_sum, unroll=min(4, n_blocks))
    acc_sc[HEAD_DIM:HEAD_DIM + 1, :] = row_sum

    @pl.when(jnp.min(row_sum) < MIN_ROW_SUM)
    def _():
        _attn_a_running_max(qk, keys, values_t, kc_ref, vtc_ref, k_ref, acc_sc, s0_sc, s1_sc, n_blocks, nq)

    o = (acc_sc[:HEAD_DIM, :] / acc_sc[HEAD_DIM:HEAD_DIM + 1, :]).T
    o_ref[0] = jnp.concatenate([o[g * tq:(g + 1) * tq] for g in range(G_A)], axis=1).astype(_BF16)


def _attn_a_running_max(qk, keys, values_t, kc_ref, vtc_ref, k_ref, acc_sc, s0_sc, s1_sc, n_blocks, nq):
    def softmax_pv(s, vb, m_prev):
        m_new = jnp.maximum(m_prev, s.max(axis=0, keepdims=True))
        alpha = jnp.exp2(m_prev - m_new)
        p = jnp.exp2(s - m_new)
        pv = jnp.dot(vb, p.astype(_BF16), preferred_element_type=_F32)
        acc_sc[...] = alpha * acc_sc[...] + pv
        return m_new

    acc_sc[...] = jnp.zeros_like(acc_sc)
    m0 = jnp.full((1, nq), NEG_INF, _F32)
    s0_sc[...] = qk(keys(k_ref, 0))
    m = softmax_pv(qk(kc_ref[0, 0]), vtc_ref[0, 0], m0)

    def pair(j0, m, last):
        s1_sc[...] = qk(keys(k_ref, j0 + 1))
        m = softmax_pv(s0_sc[...], values_t(j0), m)
        if not last:
            s0_sc[...] = qk(keys(k_ref, j0 + 2))
        return softmax_pv(s1_sc[...], values_t(j0 + 1), m)

    m = lax.fori_loop(0, n_blocks // 2 - 1, lambda jj, m: pair(2 * jj, m, False), m)
    pair(n_blocks - 2, m, True)


def _attn_a(qkv, vt, qkv_ctx, vt_ctx, bounds, layer, tq, tk):
    bsz, _, length, _ = qkv.shape
    n_ctx = qkv_ctx.shape[2] // bsz
    tq = min(tq, length)
    tk = min(tk, length // 2)
    assert length % (2 * tk) == 0
    return pl.pallas_call(
        functools.partial(_attn_a_kernel, tk=tk, layer=layer),
        grid=(bsz, KV_A, length // tq),
        in_specs=[pl.BlockSpec(memory_space=pltpu.SMEM),
                  _query_spec(tq, QA0, G_A), _head_spec(length, KA0),
                  pl.BlockSpec((1, 1, VT_ROWS, length), lambda b, h, i: (b, h, 0, 0)),
                  _ctx_head_spec(n_ctx, KA0),
                  pl.BlockSpec((1, 1, VT_ROWS, n_ctx), lambda b, h, i: (0, h, 0, b))],
        out_specs=pl.BlockSpec((1, tq, G_A * HEAD_DIM), lambda b, h, i: (b, i, h)),
        out_shape=jax.ShapeDtypeStruct((bsz, length, H_A * HEAD_DIM), _BF16),
        scratch_shapes=[pltpu.VMEM((VT_ROWS, G_A * tq), _F32),
                        pltpu.VMEM((tk, G_A * tq), _F32), pltpu.VMEM((tk, G_A * tq), _F32)],
        compiler_params=_cparams("parallel", "parallel", "parallel"),
        name="attn_global",
    )(bounds, qkv, qkv, vt, qkv_ctx, vt_ctx)


def _attn_b_kernel(bound_ref, q_ref, k_ref, v_ref, kc_ref, vc_ref, bias_ref, o_ref, *, layer):
    n_rows = k_ref.shape[2] // GRID_W
    tq = NB_QROWS * GRID_W
    nk = NB_KROWS * GRID_W
    tiles = q_ref.shape[2] // tq
    n_tiles = pl.num_programs(2) * tiles
    bound = bound_ref[layer, 1]
    units = []
    for i in range(tiles):
        t = pl.program_id(2) * tiles + i
        ks = jnp.clip(t * NB_QROWS - NA_ROWS // 2, 0, n_rows - NB_KROWS)
        off = pl.multiple_of(ks * GRID_W, GRID_W)
        variant = jnp.where(t == 0, 0, jnp.where(t == n_tiles - 1, 2, 1))
        for hh in range(NB_HEADS_PER_STEP):
            kvs = [(k_ref[0, hh, pl.ds(off, nk), :], v_ref[0, hh, pl.ds(off, nk), :],
                    bias_ref[0, hh, variant], None),
                   (kc_ref[0, hh], vc_ref[0, hh], -bound, None)]
            units.append((q_ref[0, hh, i * tq:(i + 1) * tq, :], kvs, None))

    def write(outs):
        for i in range(tiles):
            heads = outs[i * NB_HEADS_PER_STEP:(i + 1) * NB_HEADS_PER_STEP]
            o_ref[0, i * tq:(i + 1) * tq, :] = jnp.concatenate([o.T for o in heads], axis=1).astype(_BF16)

    outs, min_sum = _attend_units_shifted(units, None)
    write(outs)

    @pl.when(min_sum < MIN_ROW_SUM)
    def _():
        write(_attend_units(units))


def _nb_bias_tables(rpb, n_rows, shift):
    kh = NA_ROWS
    c = np.arange(GRID_W)[:, None, None]
    j = np.arange(2 * NA_COLS - 1)[None, :, None]
    ck = np.arange(GRID_W)[None, None, :]
    col_onehot = (j == ck - c + (NA_COLS - 1)).astype(np.float32)
    cs = np.clip(c - NA_COLS // 2, 0, GRID_W - NA_COLS)
    col_ok = ((ck >= cs) & (ck < cs + NA_COLS))[:, 0, :]
    blocks = jnp.einsum("lhrj,cjk->lhrkc", rpb * LOG2_E, col_onehot, precision=lax.Precision.HIGHEST)
    blocks = blocks - shift[:, None, None, None, None]
    blocks = jnp.where(col_ok.T, blocks, NEG_INF).astype(_F32)
    masked = jnp.full(blocks.shape[:2] + (GRID_W, GRID_W), NEG_INF, _F32)
    tabs = []
    for r0, ks in ((0, 0), (NB_QROWS, 0), (n_rows - NB_QROWS, n_rows - NB_KROWS)):
        key_rows = []
        for i in range(NB_KROWS):
            row = []
            for a in range(NB_QROWS):
                rq, rk = r0 + a, ks + i
                rs = min(max(rq - kh // 2, 0), n_rows - kh)
                row.append(blocks[:, :, rk - rq + (NA_ROWS - 1)] if rs <= rk < rs + kh else masked)
            key_rows.append(jnp.concatenate(row, axis=-1))
        tabs.append(jnp.concatenate(key_rows, axis=-2))
    return jnp.stack(tabs, axis=2)


def _attn_b(qkv, qkv_ctx, bias_tabs, bounds, layer, tiles):
    bsz, _, length, _ = qkv.shape
    n_ctx = qkv_ctx.shape[2] // bsz
    tq = min(tiles * NB_QROWS * GRID_W, length)
    assert length % tq == 0
    hps = NB_HEADS_PER_STEP
    return pl.pallas_call(
        functools.partial(_attn_b_kernel, layer=layer),
        grid=(bsz, H_B // hps, length // tq),
        in_specs=[pl.BlockSpec(memory_space=pltpu.SMEM),
                  _query_spec(tq, QB0, hps), _head_spec(length, KB0, hps), _head_spec(length, VB0, hps),
                  _ctx_head_spec(n_ctx, KB0, hps), _ctx_head_spec(n_ctx, VB0, hps),
                  pl.BlockSpec((1, hps) + bias_tabs.shape[2:], lambda b, h, t: (layer, h, 0, 0, 0))],
        out_specs=pl.BlockSpec((1, tq, hps * HEAD_DIM), lambda b, h, t: (b, t, h)),
        out_shape=jax.ShapeDtypeStruct((bsz, length, H_B * HEAD_DIM), _BF16),
        compiler_params=_cparams("parallel", "parallel", "parallel"),
        name="attn_neighbourhood",
    )(bounds, qkv, qkv, qkv, qkv_ctx, qkv_ctx, bias_tabs)


def _sink2_row(sink):
    return jnp.concatenate([sink[g:g + 1, :] for g in range(G_C)], axis=1) * LOG2_E


def _attn_c_kernel(bound_ref, q_ref, k_ref, v_ref, kc_ref, vc_ref, sink_ref, o_ref, *, layer):
    t = pl.program_id(2)
    tq = q_ref.shape[2]
    length = k_ref.shape[2]
    band = 3 * WINDOW
    nq = G_C * WINDOW
    kc = kc_ref[0, 0]
    vc = vc_ref[0, 0]
    sink2 = _sink2_row(sink_ref[0, 0])
    rel = (lax.broadcasted_iota(jnp.int32, (band, nq), 1) % WINDOW
           - lax.broadcasted_iota(jnp.int32, (band, nq), 0))
    units = []
    n_sub = tq // WINDOW
    keep_centred = jnp.abs(rel + WINDOW) <= WINDOW
    for n in range(n_sub):
        i0 = t * tq + n * WINDOW
        start = pl.multiple_of(jnp.clip(i0 - WINDOW, 0, length - band), WINDOW)
        kb = k_ref[0, 0, pl.ds(start, band), :]
        vb = v_ref[0, 0, pl.ds(start, band), :]
        q3 = q_ref[0, :, n * WINDOW:(n + 1) * WINDOW, :].reshape(nq, HEAD_DIM)
        keep = keep_centred if 0 < n < n_sub - 1 else jnp.abs(rel + (i0 - start)) <= WINDOW
        units.append((q3, [(kb, vb, None, keep), (kc, vc, None, None)], sink2))

    def write(outs):
        for n, o_t in enumerate(outs):
            o = o_t.T
            o_ref[0, n * WINDOW:(n + 1) * WINDOW, :] = jnp.concatenate(
                [o[g * WINDOW:(g + 1) * WINDOW] for g in range(G_C)], axis=1).astype(_BF16)

    outs, min_sum = _attend_units_shifted(units, bound_ref[layer, 2])
    write(outs)

    @pl.when(min_sum < MIN_ROW_SUM)
    def _():
        write(_attend_units(units))


def _attn_c(qkv, qkv_ctx, sink_tabs, bounds, layer, tq):
    bsz, _, length, _ = qkv.shape
    n_ctx = qkv_ctx.shape[2] // bsz
    tq = min(tq, length)
    return pl.pallas_call(
        functools.partial(_attn_c_kernel, layer=layer),
        grid=(bsz, KV_C, length // tq),
        in_specs=[pl.BlockSpec(memory_space=pltpu.SMEM),
                  _query_spec(tq, QC0, G_C), _head_spec(length, KC0), _head_spec(length, VC0),
                  _ctx_head_spec(n_ctx, KC0), _ctx_head_spec(n_ctx, VC0),
                  pl.BlockSpec((1, 1, G_C, HEAD_DIM), lambda b, h, t: (layer, h, 0, 0))],
        out_specs=pl.BlockSpec((1, tq, G_C * HEAD_DIM), lambda b, h, t: (b, t, h)),
        out_shape=jax.ShapeDtypeStruct((bsz, length, H_C * HEAD_DIM), _BF16),
        compiler_params=_cparams("parallel", "parallel", "parallel"),
        name="attn_window",
    )(bounds, qkv, qkv, qkv, qkv_ctx, qkv_ctx, sink_tabs)


def _attn_ctx_kernel(qkv_ref, sink_ref, ya_ref, yb_ref, yc_ref):
    n = qkv_ref.shape[2]

    def head(u):
        return qkv_ref[0, u]

    def heads(u0, g):
        return qkv_ref[0, u0:u0 + g].reshape(g * n, HEAD_DIM)

    for kv in range(KV_A):
        o = _attend_t(heads(QA0 + kv * G_A, G_A), [(head(KA0 + kv), head(VA0 + kv), None, None)]).T
        for g in range(G_A):
            u = kv * G_A + g
            ya_ref[0, :, u * HEAD_DIM:(u + 1) * HEAD_DIM] = o[g * n:(g + 1) * n].astype(_BF16)
    for hb in range(H_B):
        o = _attend_t(head(QB0 + hb), [(head(KB0 + hb), head(VB0 + hb), None, None)]).T
        yb_ref[0, :, hb * HEAD_DIM:(hb + 1) * HEAD_DIM] = o.astype(_BF16)
    for kv in range(KV_C):
        sink = sink_ref[0, kv]
        sink2 = jnp.concatenate(
            [jnp.broadcast_to(sink[g:g + 1, :1], (1, n)) for g in range(G_C)], axis=1) * LOG2_E
        o = _attend_t(heads(QC0 + kv * G_C, G_C), [(head(KC0 + kv), head(VC0 + kv), None, None)], sink2).T
        for g in range(G_C):
            u = kv * G_C + g
            yc_ref[0, :, u * HEAD_DIM:(u + 1) * HEAD_DIM] = o[g * n:(g + 1) * n].astype(_BF16)


def _attn_ctx(qkv_ctx, sink_tabs, layer, bsz):
    n_heads, rows = qkv_ctx.shape[1:3]
    n = rows // bsz
    widths = (H_A * HEAD_DIM, H_B * HEAD_DIM, H_C * HEAD_DIM)
    return pl.pallas_call(
        _attn_ctx_kernel,
        grid=(bsz,),
        in_specs=[pl.BlockSpec((1, n_heads, n, HEAD_DIM), lambda b: (0, 0, b, 0)),
                  pl.BlockSpec((1,) + sink_tabs.shape[1:], lambda b: (layer, 0, 0, 0))],
        out_specs=[pl.BlockSpec((1, n, w), lambda b: (0, b, 0)) for w in widths],
        out_shape=[jax.ShapeDtypeStruct((1, rows, w), _BF16) for w in widths],
        compiler_params=_cparams("parallel"),
        name="attn_ctx",
    )(qkv_ctx, sink_tabs)


def _outproj_kernel(ya_ref, yb_ref, yc_ref, x_ref, mod_ref, g_ref, w_ref, xo_ref, h_ref):
    tm, d = x_ref.shape[1:]
    mod = mod_ref[0, 0]
    chunks = [slice(m0, min(m0 + OUTPROJ_ROW_CHUNK, tm)) for m0 in range(0, tm, OUTPROJ_ROW_CHUNK)]

    def project(rows):
        acc = None
        r0 = 0
        for y_ref in (ya_ref, yb_ref, yc_ref):
            r1 = r0 + y_ref.shape[-1]
            part = jnp.dot(y_ref[0, rows, :], w_ref[0, r0:r1, :], preferred_element_type=_F32)
            acc = part if acc is None else acc + part
            r0 = r1
        return acc

    acc_next = project(chunks[0])
    for i, rows in enumerate(chunks):
        acc = acc_next
        if i + 1 < len(chunks):
            acc_next = project(chunks[i + 1])
        xn = x_ref[0, rows, :] + mod[:, 2 * d:3 * d] * acc
        xo_ref[0, rows, :] = xn
        h_ref[0, rows, :] = _norm_mod(xn, g_ref[0], mod[:, 3 * d:4 * d],
                                      mod[:, 4 * d:5 * d]).astype(_BF16)


def _outproj(ya, yb, yc, x, mod, layer, mod_row, gain, w_out, tm):
    bsz, length, d = x.shape
    tm = min(tm, length)
    row = lambda b, i: (b, i, 0)
    return pl.pallas_call(
        _outproj_kernel,
        grid=(bsz, length // tm),
        in_specs=[pl.BlockSpec((1, tm, y.shape[-1]), row) for y in (ya, yb, yc)] + [
            pl.BlockSpec((1, tm, d), row),
            _mod_spec(mod, layer, mod_row, 2),
            pl.BlockSpec((1, 1, d), lambda b, i: (layer, 0, 0)),
            pl.BlockSpec((1,) + w_out.shape[1:], lambda b, i: (layer, 0, 0), pipeline_mode=pl.Buffered(1)),
        ],
        out_specs=[pl.BlockSpec((1, tm, d), row), pl.BlockSpec((1, tm, d), row)],
        out_shape=[jax.ShapeDtypeStruct((bsz, length, d), _F32),
                   jax.ShapeDtypeStruct((bsz, length, d), _BF16)],
        compiler_params=_cparams("parallel", "parallel"),
        name="out_proj_residual",
    )(ya, yb, yc, x, mod, gain, w_out)


def _mlp_kernel(h_ref, x_ref, mod_ref, wu_ref, wd_ref, o_ref):
    k = pl.program_id(2)
    d = x_ref.shape[-1]
    @pl.when(k == 0)
    def _():
        o_ref[...] = jnp.zeros_like(o_ref)

    u = jnp.dot(h_ref[0], wu_ref[0], preferred_element_type=_F32)
    act = jnp.square(jnp.maximum(u, 0.0)).astype(_BF16)
    o_ref[0] += jnp.dot(act, wd_ref[0], preferred_element_type=_F32)

    @pl.when(k == pl.num_programs(2) - 1)
    def _():
        o_ref[0] = x_ref[0] + mod_ref[0, 0, :, 5 * d:6 * d] * o_ref[0]


def _mlp(h, x, mod, layer, mod_row, w_up, w_down, tm, tf):
    bsz, length, d = x.shape
    d_ff = w_up.shape[-1]
    tm = min(tm, length)
    tf = min(tf, d_ff)
    row = lambda b, i, k: (b, i, 0)
    return pl.pallas_call(
        _mlp_kernel,
        grid=(bsz, length // tm, d_ff // tf),
        in_specs=[pl.BlockSpec((1, tm, d), row, pipeline_mode=pl.Buffered(1)),
                  pl.BlockSpec((1, tm, d), row, pipeline_mode=pl.Buffered(1)),
                  _mod_spec(mod, layer, mod_row, 3),
                  pl.BlockSpec((1, d, tf), lambda b, i, k: (layer, 0, k)),
                  pl.BlockSpec((1, tf, d), lambda b, i, k: (layer, k, 0))],
        out_specs=pl.BlockSpec((1, tm, d), row),
        out_shape=jax.ShapeDtypeStruct((bsz, length, d), _F32),
        compiler_params=_cparams("parallel", "parallel", "arbitrary"),
        name="mlp_sq_relu",
    )(h, x, mod, w_up, w_down)


def _logit_bounds(q_norm, k_norm, rel_pos_bias, sink_logits):
    gq = jnp.max(jnp.abs(q_norm), axis=-1)
    gk = jnp.max(jnp.abs(k_norm), axis=-1)
    qk_bound = HEAD_DIM * LOGIT2_SCALE * BOUND_MARGIN * gq * gk + BOUND_MARGIN - 1.0
    bias_max = jnp.maximum(jnp.max(rel_pos_bias, axis=(1, 2, 3)), 0.0) * LOG2_E
    sink_max = jnp.max(sink_logits, axis=-1) * LOG2_E
    return jnp.stack([qk_bound[:, 0], qk_bound[:, 1] + bias_max,
                      jnp.maximum(qk_bound[:, 2], sink_max)], axis=-1).astype(_F32)


def _rope_tables(length):
    t = np.arange(length)
    nf = HEAD_DIM // 4
    inv = ROPE_THETA ** (-jnp.arange(nf, dtype=_F32) / nf)
    ar = jnp.asarray(t // GRID_W, _F32)[:, None] * inv
    ac = jnp.asarray(t % GRID_W, _F32)[:, None] * inv
    ang = jnp.concatenate([ar, ar, ac, ac], axis=-1)
    cos, sin = jnp.cos(ang), jnp.sin(ang)
    first = (np.arange(HEAD_DIM) // nf) % 2 == 0
    return cos, jnp.where(first, -sin, 0.0), jnp.where(first, 0.0, sin)


def kernel(x, c, ctx, c_ctx, w_mod, b_mod, norm_attn, norm_mlp, w_in, q_norm, k_norm,
           rel_pos_bias, sink_logits, w_out, w_up, w_down):
    bsz, length, d = x.shape
    depth = w_mod.shape[0]
    assert length % (NB_QROWS * GRID_W) == 0 and length // GRID_W >= NB_KROWS
    assert w_in.shape[-1] == N_HEADS_IN * HEAD_DIM and w_out.shape[1] == N_HEADS_OUT * HEAD_DIM

    mod_rows = 8
    c_rows = jnp.zeros((mod_rows, d), _F32).at[:bsz].set(c).at[bsz].set(c_ctx)
    mod = _modulation(c_rows, w_mod, b_mod).reshape(depth, mod_rows, 1, N_MOD * d)

    w_in_b, w_out_b = w_in.astype(_BF16), w_out.astype(_BF16)
    w_up_b, w_down_b = w_up.astype(_BF16), w_down.astype(_BF16)
    g_attn = norm_attn.reshape(depth, 1, d)
    g_mlp = norm_mlp.reshape(depth, 1, d)
    rope_tabs = _rope_tables(length)
    bounds = _logit_bounds(q_norm, k_norm, rel_pos_bias, sink_logits)
    bias_tabs = _nb_bias_tables(rel_pos_bias, length // GRID_W, bounds[:, 1])
    sink_tabs = jnp.broadcast_to(
        sink_logits.reshape(depth, KV_C, G_C, 1), (depth, KV_C, G_C, HEAD_DIM)).astype(_F32)

    cs = ctx.reshape(1, bsz * ctx.shape[1], d)
    for l in range(depth):
        last = l == depth - 1
        qkv, vt = _qkv_proj(x, mod, l, None, g_attn, w_in_b, q_norm, k_norm, rope_tabs, tm=256)
        qkv_c, vt_c = _qkv_proj(cs, mod, l, bsz, g_attn, w_in_b, q_norm, k_norm, None, tm=512)
        ya = _attn_a(qkv, vt, qkv_c, vt_c, bounds, l, tq=1024, tk=1024)
        yb = _attn_b(qkv, qkv_c, bias_tabs, bounds, l, tiles=8)
        yc = _attn_c(qkv, qkv_c, sink_tabs, bounds, l, tq=4096)
        x, h = _outproj(ya, yb, yc, x, mod, l, None, g_mlp, w_out_b, tm=512)
        x = _mlp(h, x, mod, l, None, w_up_b, w_down_b, tm=512, tf=2048)
        if not last:
            ya_c, yb_c, yc_c = _attn_ctx(qkv_c, sink_tabs, l, bsz)
            cs, hc = _outproj(ya_c, yb_c, yc_c, cs, mod, l, bsz, g_mlp, w_out_b, tm=512)
            cs = _mlp(hc, cs, mod, l, bsz, w_up_b, w_down_b, tm=512, tf=1024)
    return x
```
